```python
import math
import jax, jax.numpy as jnp
from jax import lax
import numpy as np

D_MODEL = 1024
BATCH = 4
SEQ = 8192
DEPTH = 1
DEC_BATCH = 8
DEC_SEQ = 2048
PAST_LEN = 128

D_MIX = D_MODEL
HY_WIDTH = D_MIX // 2
HY_STREAMS = 3
SHORT_CONV = 3
FILT_BANDS = 16
FILT_EMB = 1 + 2 * FILT_BANDS
FILT_HIDDEN = 64
DECAY_TARGET = 1e-2
FAST_DECAY_PCT = 0.3
SLOW_DECAY_PCT = 1.5
N_HEADS = 8
QK_NOPE = 64
QK_ROPE = 32
QK_DIM = QK_NOPE + QK_ROPE
V_DIM = 64
MLA_WIDTH = N_HEADS * V_DIM
Q_LORA = 256
KV_LORA = 128
ROPE_THETA = 10000.0
Q_BLOCK = 128
D_FF = 4 * D_MODEL
PLE_DIM = 256
EPS = 1e-6
IN_COLS = HY_STREAMS * HY_WIDTH + Q_LORA + KV_LORA + QK_ROPE

kernel_name = "hymba_hyena_mla_encoder"

F32 = jnp.float32


def _rmsnorm(x, w):
    x32 = x.astype(F32)
    y = x32 * lax.rsqrt(jnp.mean(x32 * x32, axis=-1, keepdims=True) + EPS)
    return (y * w.astype(F32)).astype(x.dtype)


def _short_conv(u, w, b):
    L = u.shape[1]
    pad = SHORT_CONV // 2
    up = jnp.pad(u, ((0, 0), (pad, SHORT_CONV - 1 - pad), (0, 0)))
    out = b
    for j in range(SHORT_CONV):
        out = out + up[:, j:j + L] * w[j]
    return out


def _hyena_filters(L, filt_w1, filt_b1, filt_freq, filt_w2, filt_b2, filt_w3):
    t = jnp.linspace(0.0, 1.0, L, dtype=F32)[:, None]
    w = (2.0 * math.pi) * jnp.arange(L, dtype=F32)[:, None] / L
    f = jnp.linspace(1e-4, FILT_BANDS - 1, FILT_BANDS, dtype=F32)[None, :]
    z = jnp.concatenate([t, jnp.cos(f * w), -jnp.sin(f * w)], axis=-1)
    freq = filt_freq.astype(F32)
    h = jnp.sin(freq * (z @ filt_w1.astype(F32) + filt_b1.astype(F32)))
    h = jnp.sin(freq * (h @ filt_w2.astype(F32) + filt_b2.astype(F32)))
    h = h @ filt_w3.astype(F32)
    min_decay = math.log(DECAY_TARGET) / FAST_DECAY_PCT
    max_decay = math.log(DECAY_TARGET) / SLOW_DECAY_PCT
    deltas = jnp.abs(jnp.linspace(min_decay, max_decay, HY_WIDTH, dtype=F32))
    decay = jnp.exp(-t * deltas)
    return h[:, :HY_WIDTH] * decay, h[:, HY_WIDTH:] * decay


def _hyena(hy, conv_w, conv_b, filt_w1, filt_b1, filt_freq, filt_w2, filt_b2, filt_w3, filt_bias):
    L = hy.shape[1]
    u = _short_conv(hy, conv_w, conv_b)
    x0, x1, v = jnp.split(u, HY_STREAMS, axis=-1)
    h_fwd, h_bwd = _hyena_filters(L, filt_w1, filt_b1, filt_freq, filt_w2, filt_b2, filt_w3)
    kern = jnp.concatenate([h_fwd, h_bwd[::-1]], axis=0)
    kf = jnp.fft.rfft(kern, axis=0)
    z = (v * x1).astype(F32)
    zf = jnp.fft.rfft(z, n=2 * L, axis=1)
    y = jnp.fft.irfft(zf * kf[None], n=2 * L, axis=1)[:, :L] + z * filt_bias.astype(F32)
    return (x0.astype(F32) * y).astype(hy.dtype)


def _rope(x, L):
    half = QK_ROPE // 2
    inv = ROPE_THETA ** (-(2.0 * jnp.arange(half, dtype=F32)) / QK_ROPE)
    ang = jnp.arange(L, dtype=F32)[:, None] * inv[None, :]
    cos = jnp.cos(ang)[None, :, None, :]
    sin = jnp.sin(ang)[None, :, None, :]
    x32 = x.astype(F32)
    a, b = x32[..., :half], x32[..., half:]
    return jnp.concatenate([a * cos - b * sin, a * sin + b * cos], axis=-1).astype(x.dtype)


def _mla(q_lat, kv_lat, k_rope, q_a_norm_w, w_q_b, kv_a_norm_w, w_kv_b, q_norm_w, k_norm_w):
    B, L = q_lat.shape[0], q_lat.shape[1]
    q = (_rmsnorm(q_lat, q_a_norm_w) @ w_q_b).reshape(B, L, N_HEADS, QK_DIM)
    kv = (_rmsnorm(kv_lat, kv_a_norm_w) @ w_kv_b).reshape(B, L, N_HEADS, QK_NOPE + V_DIM)
    k_nope, v = kv[..., :QK_NOPE], kv[..., QK_NOPE:]
    k = jnp.concatenate([k_nope, jnp.broadcast_to(k_rope[:, :, None, :], (B, L, N_HEADS, QK_ROPE))], axis=-1)
    q = _rmsnorm(q, q_norm_w)
    k = _rmsnorm(k, k_norm_w)
    q = jnp.concatenate([q[..., :QK_NOPE], _rope(q[..., QK_NOPE:], L)], axis=-1)
    k = jnp.concatenate([k[..., :QK_NOPE], _rope(k[..., QK_NOPE:], L)], axis=-1)
    nb = L // Q_BLOCK
    qb = q.reshape(B, nb, Q_BLOCK, N_HEADS, QK_DIM).transpose(1, 0, 2, 3, 4)
    scale = QK_DIM ** -0.5

    def attend(qi):
        s = jnp.einsum('bqhd,bkhd->bhqk', qi, k, preferred_element_type=F32) * scale
        p = jax.nn.softmax(s, axis=-1)
        return jnp.einsum('bhqk,bkhv->bqhv', p.astype(v.dtype), v)

    o = lax.map(attend, qb)
    return o.transpose(1, 0, 2, 3, 4).reshape(B, L, MLA_WIDTH)


def _layer(x, p_l, attn_norm_w, w_in, conv_w, conv_b, filt_w1, filt_b1, filt_freq, filt_w2,
           filt_b2, filt_w3, filt_bias, q_a_norm_w, w_q_b, kv_a_norm_w, w_kv_b, q_norm_w,
           k_norm_w, hy_out_norm_w, mla_out_norm_w, w_out, mlp_norm_w, w_mlp1, w_mlp2,
           w_ple_gate, w_ple_proj, ple_norm_w):
    xn = _rmsnorm(x, attn_norm_w)
    proj = xn @ w_in
    o1 = HY_STREAMS * HY_WIDTH
    o2 = o1 + Q_LORA
    o3 = o2 + KV_LORA
    hy, q_lat, kv_lat, k_rope = proj[..., :o1], proj[..., o1:o2], proj[..., o2:o3], proj[..., o3:]
    y_hy = _hyena(hy, conv_w, conv_b, filt_w1, filt_b1, filt_freq, filt_w2, filt_b2, filt_w3, filt_bias)
    y_mla = _mla(q_lat, kv_lat, k_rope, q_a_norm_w, w_q_b, kv_a_norm_w, w_kv_b, q_norm_w, k_norm_w)
    mix = jnp.concatenate([_rmsnorm(y_hy, hy_out_norm_w), _rmsnorm(y_mla, mla_out_norm_w)], axis=-1)
    x = x + mix @ w_out
    h = _rmsnorm(x, mlp_norm_w) @ w_mlp1
    x = x + jnp.square(jax.nn.relu(h)) @ w_mlp2
    gate = jax.nn.sigmoid(x @ w_ple_gate)
    x = x + gate * _rmsnorm(p_l @ w_ple_proj, ple_norm_w)
    return x


def _trunk(x, p, attn_norm_w, w_in, conv_w, conv_b, filt_w1, filt_b1, filt_freq, filt_w2,
           filt_b2, filt_w3, filt_bias, q_a_norm_w, w_q_b, kv_a_norm_w, w_kv_b, q_norm_w,
           k_norm_w, hy_out_norm_w, mla_out_norm_w, w_out, mlp_norm_w, w_mlp1, w_mlp2,
           w_ple_gate, w_ple_proj, ple_norm_w):
    for i in range(DEPTH):
        x = _layer(x, p[i], attn_norm_w[i], w_in[i], conv_w[i], conv_b[i], filt_w1[i], filt_b1[i],
                   filt_freq[i], filt_w2[i], filt_b2[i], filt_w3[i], filt_bias[i], q_a_norm_w[i],
                   w_q_b[i], kv_a_norm_w[i], w_kv_b[i], q_norm_w[i], k_norm_w[i], hy_out_norm_w[i],
                   mla_out_norm_w[i], w_out[i], mlp_norm_w[i], w_mlp1[i], w_mlp2[i], w_ple_gate[i],
                   w_ple_proj[i], ple_norm_w[i])
    return x


def setup_inputs(seed: int = 0) -> dict:
    key = jax.random.key(seed)
    ks = jax.random.split(key, 40)

    def nrm(k, shape, scale=1.0):
        return jax.random.normal(k, shape, dtype=F32) * scale

    def gain(k, n):
        return 1.0 + 0.02 * jax.random.normal(k, (DEPTH, n), dtype=F32)

    return {
        "x_prompt": nrm(ks[0], (BATCH, SEQ, D_MODEL)),
        "x_sample": nrm(ks[1], (DEC_BATCH, DEC_SEQ, D_MODEL)),
        "p_prompt": nrm(ks[2], (DEPTH, BATCH, SEQ, PLE_DIM)),
        "p_sample": nrm(ks[3], (DEPTH, DEC_BATCH, DEC_SEQ, PLE_DIM)),
        "attn_norm_w": gain(ks[4], D_MODEL),
        "w_in": nrm(ks[5], (DEPTH, D_MODEL, IN_COLS), D_MODEL ** -0.5),
        "conv_w": nrm(ks[6], (DEPTH, SHORT_CONV, HY_STREAMS * HY_WIDTH), SHORT_CONV ** -0.5),
        "conv_b": nrm(ks[7], (DEPTH, HY_STREAMS * HY_WIDTH), 0.02),
        "filt_w1": nrm(ks[8], (DEPTH, FILT_EMB, FILT_HIDDEN), FILT_EMB ** -0.5),
        "filt_b1": nrm(ks[9], (DEPTH, FILT_HIDDEN), 0.02),
        "filt_freq": gain(ks[10], FILT_HIDDEN),
        "filt_w2": nrm(ks[11], (DEPTH, FILT_HIDDEN, FILT_HIDDEN), FILT_HIDDEN ** -0.5),
        "filt_b2": nrm(ks[12], (DEPTH, FILT_HIDDEN), 0.02),
        "filt_w3": nrm(ks[13], (DEPTH, FILT_HIDDEN, 2 * HY_WIDTH), FILT_HIDDEN ** -0.5),
        "filt_bias": nrm(ks[14], (DEPTH, HY_WIDTH), 0.5),
        "q_a_norm_w": gain(ks[15], Q_LORA),
        "w_q_b": nrm(ks[16], (DEPTH, Q_LORA, N_HEADS * QK_DIM), Q_LORA ** -0.5),
        "kv_a_norm_w": gain(ks[17], KV_LORA),
        "w_kv_b": nrm(ks[18], (DEPTH, KV_LORA, N_HEADS * (QK_NOPE + V_DIM)), KV_LORA ** -0.5),
        "q_norm_w": gain(ks[19], QK_DIM),
        "k_norm_w": gain(ks[20], QK_DIM),
        "hy_out_norm_w": gain(ks[21], HY_WIDTH),
        "mla_out_norm_w": gain(ks[22], MLA_WIDTH),
        "w_out": nrm(ks[23], (DEPTH, HY_WIDTH + MLA_WIDTH, D_MODEL), (HY_WIDTH + MLA_WIDTH) ** -0.5),
        "mlp_norm_w": gain(ks[24], D_MODEL),
        "w_mlp1": nrm(ks[25], (DEPTH, D_MODEL, D_FF), D_MODEL ** -0.5),
        "w_mlp2": nrm(ks[26], (DEPTH, D_FF, D_MODEL), D_FF ** -0.5),
        "w_ple_gate": nrm(ks[27], (DEPTH, D_MODEL, D_MODEL), D_MODEL ** -0.5),
        "w_ple_proj": nrm(ks[28], (DEPTH, PLE_DIM, D_MODEL), PLE_DIM ** -0.5),
        "ple_norm_w": gain(ks[29], D_MODEL),
    }


def reference(x_prompt, x_sample, p_prompt, p_sample, attn_norm_w, w_in, conv_w, conv_b,
              filt_w1, filt_b1, filt_freq, filt_w2, filt_b2, filt_w3, filt_bias, q_a_norm_w,
              w_q_b, kv_a_norm_w, w_kv_b, q_norm_w, k_norm_w, hy_out_norm_w, mla_out_norm_w,
              w_out, mlp_norm_w, w_mlp1, w_mlp2, w_ple_gate, w_ple_proj, ple_norm_w):
    y_prompt = _trunk(x_prompt, p_prompt, attn_norm_w, w_in, conv_w, conv_b, filt_w1, filt_b1,
                      filt_freq, filt_w2, filt_b2, filt_w3, filt_bias, q_a_norm_w, w_q_b,
                      kv_a_norm_w, w_kv_b, q_norm_w, k_norm_w, hy_out_norm_w, mla_out_norm_w,
                      w_out, mlp_norm_w, w_mlp1, w_mlp2, w_ple_gate, w_ple_proj, ple_norm_w)
    y_sample = _trunk(x_sample, p_sample, attn_norm_w, w_in, conv_w, conv_b, filt_w1, filt_b1,
                      filt_freq, filt_w2, filt_b2, filt_w3, filt_bias, q_a_norm_w, w_q_b,
                      kv_a_norm_w, w_kv_b, q_norm_w, k_norm_w, hy_out_norm_w, mla_out_norm_w,
                      w_out, mlp_norm_w, w_mlp1, w_mlp2, w_ple_gate, w_ple_proj, ple_norm_w)
    return (y_prompt, y_sample)
```

```python
import functools
import math

import jax
import jax.numpy as jnp
from jax import lax
from jax.experimental import pallas as pl
from jax.experimental.pallas import tpu as pltpu

F32 = jnp.float32
BF16 = jnp.bfloat16
HIGHEST = lax.Precision.HIGHEST

LANES = 128
SUBLANES = 8
VMEM_LIMIT = 56 * 1024 * 1024

HY_WIDTH = 512
HY_STREAMS = 3
HY3 = HY_STREAMS * HY_WIDTH
FILT_BANDS = 16
FILT_HIDDEN = 64
DECAY_TARGET = 1e-2
FAST_DECAY_PCT = 0.3
SLOW_DECAY_PCT = 1.5
N_HEADS = 8
QK_NOPE = 64
QK_ROPE = 32
QK_DIM = QK_NOPE + QK_ROPE
V_DIM = 64
Q_LORA = 256
KV_LORA = 128
ROPE_THETA = 10000.0
EPS = 1e-6
HEAD_PAD = LANES
HALF_ROPE = QK_ROPE // 2


def _rms(x, w):
    ms = jnp.mean(x * x, axis=-1, keepdims=True)
    return x * lax.rsqrt(ms + EPS) * w


def _params(*sem):
    return pltpu.CompilerParams(dimension_semantics=sem, vmem_limit_bytes=VMEM_LIMIT)


def _const_spec(shape):
    nd = len(shape)
    return pl.BlockSpec(shape, lambda *_: (0,) * nd)


def _inproj_body(x_ref, anw_ref, win_ref, qaw_ref, wqb_ref, kvaw_ref, wkvb_ref, qnw_ref, knw_ref,
                 cos_ref, sin_ref, hy_ref, q_ref, k_ref, v_ref):
    xn = _rms(x_ref[...], anw_ref[...])
    proj = jnp.dot(xn.astype(BF16), win_ref[...], preferred_element_type=F32)
    hy_ref[...] = proj[:, :HY3]
    o2 = HY3 + Q_LORA
    o3 = o2 + KV_LORA
    q_lat, kv_lat, krope = proj[:, HY3:o2], proj[:, o2:o3], proj[:, o3:]
    qf = jnp.dot(_rms(q_lat, qaw_ref[...]).astype(BF16), wqb_ref[...], preferred_element_type=F32)
    kvf = jnp.dot(_rms(kv_lat, kvaw_ref[...]).astype(BF16), wkvb_ref[...], preferred_element_type=F32)
    kw = N_HEADS * HEAD_PAD
    v_ref[...] = kvf[:, kw:].astype(BF16)

    cosf, sinf = cos_ref[...], sin_ref[...]
    lane = lax.broadcasted_iota(jnp.int32, cosf.shape, 1)
    first = lane < QK_NOPE + HALF_ROPE
    scale = QK_DIM ** -0.5

    def norm_rope(xh, w):
        ms = jnp.sum(xh * xh, axis=-1, keepdims=True) / QK_DIM
        y = xh * lax.rsqrt(ms + EPS) * w
        rot = jnp.where(first, pltpu.roll(y, HEAD_PAD - HALF_ROPE, 1), pltpu.roll(y, HALF_ROPE, 1))
        return y * cosf + rot * sinf

    qnw, knw = qnw_ref[...], knw_ref[...]
    for h in range(N_HEADS):
        sl = slice(h * HEAD_PAD, (h + 1) * HEAD_PAD)
        q_ref[:, sl] = (norm_rope(qf[:, sl], qnw) * scale).astype(BF16)
        k_ref[:, sl] = norm_rope(kvf[:, sl] + krope, knw).astype(BF16)


def _inproj(x, anw, win, qaw, wqb, kvaw, wkvb, qnw, knw, cosf, sinf, tm):
    B, L, D = x.shape
    kw = N_HEADS * HEAD_PAD
    vw = N_HEADS * V_DIM
    tok = lambda w: pl.BlockSpec((None, tm, w), lambda b, t: (b, t, 0))
    pos = pl.BlockSpec((tm, HEAD_PAD), lambda b, t: (t, 0))
    return pl.pallas_call(
        _inproj_body,
        grid=(B, L // tm),
        in_specs=[tok(D), _const_spec(anw.shape), _const_spec(win.shape), _const_spec(qaw.shape),
                  _const_spec(wqb.shape), _const_spec(kvaw.shape), _const_spec(wkvb.shape),
                  _const_spec(qnw.shape), _const_spec(knw.shape), pos, pos],
        out_specs=[tok(HY3), tok(kw), tok(kw), tok(vw)],
        out_shape=[jax.ShapeDtypeStruct((B, L, HY3), F32), jax.ShapeDtypeStruct((B, L, kw), BF16),
                   jax.ShapeDtypeStruct((B, L, kw), BF16), jax.ShapeDtypeStruct((B, L, vw), BF16)],
        compiler_params=_params("parallel", "parallel"),
        name="inproj",
    )(x, anw, win, qaw, wqb, kvaw, wkvb, qnw, knw, cosf, sinf)


def _gate_body(prev_ref, cur_ref, next_ref, cw_ref, cb_ref, x0_ref, z_ref):
    t = pl.program_id(1)
    nt = pl.num_programs(1)
    tl = cur_ref.shape[0]
    row = lax.broadcasted_iota(jnp.int32, (tl, HY_WIDTH), 0)
    has_prev = (t > 0).astype(F32)
    has_next = (t < nt - 1).astype(F32)

    def stream(s):
        sl = slice(s * HY_WIDTH, (s + 1) * HY_WIDTH)
        cur = cur_ref[:, sl]
        prow = prev_ref[SUBLANES - 1:SUBLANES, sl] * has_prev
        nrow = next_ref[0:1, sl] * has_next
        dn = jnp.where(row == 0, prow, pltpu.roll(cur, 1, 0))
        up = jnp.where(row == tl - 1, nrow, pltpu.roll(cur, tl - 1, 0))
        return cb_ref[:, sl] + dn * cw_ref[0:1, sl] + cur * cw_ref[1:2, sl] + up * cw_ref[2:3, sl]

    x0_ref[...] = stream(0)
    z_ref[...] = stream(2) * stream(1)


def _gate(hy, conv_w, conv_b, tl):
    B, L, _ = hy.shape
    nb = tl // SUBLANES
    last = L // SUBLANES - 1
    out = pl.BlockSpec((None, tl, HY_WIDTH), lambda b, t: (b, t, 0))
    return pl.pallas_call(
        _gate_body,
        grid=(B, L // tl),
        in_specs=[pl.BlockSpec((None, SUBLANES, HY3), lambda b, t: (b, jnp.maximum(t * nb - 1, 0), 0)),
                  pl.BlockSpec((None, tl, HY3), lambda b, t: (b, t, 0)),
                  pl.BlockSpec((None, SUBLANES, HY3), lambda b, t: (b, jnp.minimum((t + 1) * nb, last), 0)),
                  _const_spec(conv_w.shape), _const_spec(conv_b.shape)],
        out_specs=[out, out],
        out_shape=[jax.ShapeDtypeStruct((B, L, HY_WIDTH), F32)] * 2,
        compiler_params=_params("parallel", "parallel"),
        name="gate",
    )(hy, hy, hy, conv_w, conv_b)


def _filt_body(z_ref, w1_ref, b1_ref, fr_ref, w2_ref, b2_ref, w3_ref, dl_ref, o_ref):
    z = z_ref[...]
    fr = fr_ref[...]
    dot = functools.partial(jnp.dot, precision=HIGHEST, preferred_element_type=F32)
    h = jnp.sin(fr * (dot(z, w1_ref[...]) + b1_ref[...]))
    h = jnp.sin(fr * (dot(h, w2_ref[...]) + b2_ref[...]))
    h = dot(h, w3_ref[...])
    o_ref[...] = h * jnp.exp(-z[:, 0:1] * dl_ref[...])


def _filt(feat, w1, b1, fr, w2, b2, w3, deltas, tf):
    n2l = feat.shape[0]
    half = n2l // tf // 2
    return pl.pallas_call(
        _filt_body,
        grid=(n2l // tf,),
        in_specs=[pl.BlockSpec((tf, LANES), lambda t: (t, 0)), _const_spec(w1.shape), _const_spec(b1.shape),
                  _const_spec(fr.shape), _const_spec(w2.shape), _const_spec(b2.shape),
                  pl.BlockSpec((FILT_HIDDEN, HY_WIDTH), lambda t: (0, t // half)),
                  _const_spec(deltas.shape)],
        out_specs=pl.BlockSpec((tf, HY_WIDTH), lambda t: (t, 0)),
        out_shape=jax.ShapeDtypeStruct((n2l, HY_WIDTH), F32),
        compiler_params=_params("parallel"),
        name="filt",
    )(feat, w1, b1, fr, w2, b2, w3, deltas)


def _dft1_body(c_ref, x_ref, o_ref):
    o_ref[...] = jnp.dot(c_ref[...], x_ref[...], precision=HIGHEST, preferred_element_type=F32)


def _dft1(cmat, xv, ct):
    P, K, W = xv.shape
    M = cmat.shape[0]
    return pl.pallas_call(
        _dft1_body,
        grid=(P, W // ct),
        in_specs=[_const_spec(cmat.shape), pl.BlockSpec((None, K, ct), lambda p, c: (p, 0, c))],
        out_specs=pl.BlockSpec((None, M, ct), lambda p, c: (p, 0, c)),
        out_shape=jax.ShapeDtypeStruct((P, M, W), F32),
        compiler_params=_params("parallel", "parallel"),
        name="dft1",
    )(cmat, xv)


def _mid_kernel_body(m_ref, a_ref, o_ref):
    n2 = a_ref.shape[1]
    a = a_ref[...].reshape(2 * n2, a_ref.shape[2])
    x = jnp.dot(m_ref[...], a, precision=HIGHEST, preferred_element_type=F32)
    o_ref[...] = x.reshape(o_ref.shape)


def _mid_body(m_ref, mt_ref, kf_ref, a_ref, o_ref):
    n2 = a_ref.shape[1]
    a = a_ref[...].reshape(2 * n2, a_ref.shape[2])
    x = jnp.dot(m_ref[...], a, precision=HIGHEST, preferred_element_type=F32)
    xr, xi = x[:n2], x[n2:]
    kr, ki = kf_ref[0], kf_ref[1]
    y = jnp.concatenate([xr * kr - xi * ki, xr * ki + xi * kr], axis=0)
    b = jnp.dot(mt_ref[...], y, precision=HIGHEST, preferred_element_type=F32)
    o_ref[...] = b.reshape(o_ref.shape)


def _mid(mblk, mblk_t, kf, a, cw):
    P, _, n1, n2, C = a.shape
    dat = pl.BlockSpec((None, 2, None, n2, cw), lambda k, c, p: (p, 0, k, 0, c))
    mat = pl.BlockSpec((None, 2 * n2, 2 * n2), lambda k, c, p: (k, 0, 0))
    if kf is None:
        body, ins, specs = _mid_kernel_body, (mblk, a), [mat, dat]
    else:
        kspec = pl.BlockSpec((None, 2, None, n2, cw), lambda k, c, p: (0, 0, k, 0, c))
        body, ins, specs = _mid_body, (mblk, mblk_t, kf, a), [mat, mat, kspec, dat]
    return pl.pallas_call(
        body,
        grid=(n1, C // cw, P),
        in_specs=specs,
        out_specs=dat,
        out_shape=jax.ShapeDtypeStruct(a.shape, F32),
        compiler_params=_params("parallel", "parallel", "parallel"),
        name="mid",
    )(*ins)


def _dft4_body(c_ref, b_ref, x0_ref, z_ref, bias_ref, o_ref):
    y = jnp.dot(c_ref[...], b_ref[...], precision=HIGHEST, preferred_element_type=F32)
    o_ref[...] = x0_ref[...] * (y + z_ref[...] * bias_ref[...])


def _dft4(cmat, bv, x0v, zv, bias_t, ct):
    P, K, W = bv.shape
    M = cmat.shape[0]
    row = pl.BlockSpec((None, M, ct), lambda p, c: (p, 0, c))
    return pl.pallas_call(
        _dft4_body,
        grid=(P, W // ct),
        in_specs=[_const_spec(cmat.shape), pl.BlockSpec((None, K, ct), lambda p, c: (p, 0, c)), row, row,
                  _const_spec(bias_t.shape)],
        out_specs=row,
        out_shape=jax.ShapeDtypeStruct((P, M, W), F32),
        compiler_params=_params("parallel", "parallel"),
        name="dft4",
    )(cmat, bv, x0v, zv, bias_t)


def _dft_tables(n1, n2):
    n = n1 * n2
    h = n1 // 2
    i1 = jnp.arange(n1, dtype=jnp.int32)
    th = (2.0 * math.pi / n1) * ((i1[:, None] * i1[None, :]) % n1).astype(F32)
    c1, s1 = jnp.cos(th), jnp.sin(th)
    cz = jnp.block([[c1[:, :h], s1[:, :h]], [-s1[:, :h], c1[:, :h]]])
    ck = jnp.concatenate([c1, -s1], axis=0)
    c4 = jnp.block([[c1[:h], -s1[:h]], [s1[:h], c1[:h]]]) * (1.0 / n)
    k1 = jnp.arange(n1, dtype=jnp.int32)[:, None, None]
    k2 = jnp.arange(n2, dtype=jnp.int32)[None, :, None]
    i2 = jnp.arange(n2, dtype=jnp.int32)[None, None, :]
    ph = (2.0 * math.pi / n) * ((i2 * (k1 + n1 * k2)) % n).astype(F32)
    mr, mi = jnp.cos(ph), -jnp.sin(ph)
    mblk = jnp.concatenate([jnp.concatenate([mr, -mi], axis=2), jnp.concatenate([mi, mr], axis=2)], axis=1)
    return cz, ck, c4, mblk, jnp.swapaxes(mblk, 1, 2)


def _split(L):
    n = 2 * L
    n1 = 1 << ((n.bit_length() - 1) // 2)
    return n1, n // n1


def _hyena_conv(x0, z, kern, filt_bias, ct, cw):
    B, L, C = z.shape
    n1, n2 = _split(L)
    P = B // 2
    cz, ck, c4, mblk, mblk_t = _dft_tables(n1, n2)
    w = n2 * C
    ka = _dft1(ck, kern.reshape(1, n1, w), ct)
    kf = _mid(mblk, None, None, ka.reshape(1, 2, n1, n2, C), cw)
    za = _dft1(cz, z.reshape(P, n1, w), ct)
    zb = _mid(mblk, mblk_t, kf, za.reshape(P, 2, n1, n2, C), cw)
    bias_t = jnp.tile(filt_bias.reshape(1, C), (1, ct // C))
    y = _dft4(c4, zb.reshape(P, 2 * n1, w), x0.reshape(P, n1, w), z.reshape(P, n1, w), bias_t, ct)
    return y.reshape(B, L, C)


def _attn_body(q_ref, k_ref, v_ref, o_ref, *, tk):
    tq = q_ref.shape[0]
    nk = k_ref.shape[0] // tk
    q = [q_ref[:, h * HEAD_PAD:(h + 1) * HEAD_PAD] for h in range(2)]

    def step(j, carry):
        off = pl.multiple_of(j * tk, tk)
        vc = v_ref[pl.ds(off, tk), :]
        out = []
        for h in range(2):
            m, l, acc = carry[h]
            kc = k_ref[pl.ds(off, tk), h * HEAD_PAD:(h + 1) * HEAD_PAD]
            s = lax.dot_general(q[h], kc, (((1,), (1,)), ((), ())), preferred_element_type=F32)
            m_new = jnp.maximum(m, jnp.max(s, axis=-1, keepdims=True))
            p = jnp.exp(s - m_new)
            alpha = jnp.exp(m - m_new)
            l = alpha * l + jnp.sum(p, axis=-1, keepdims=True)
            acc = alpha * acc + jnp.dot(p.astype(BF16), vc, preferred_element_type=F32)
            out.append((m_new, l, acc))
        return tuple(out)

    init = tuple((jnp.full((tq, 1), -jnp.inf, F32), jnp.zeros((tq, 1), F32), jnp.zeros((tq, 2 * V_DIM), F32))
                 for _ in range(2))
    res = lax.fori_loop(0, nk, step, init)
    o0 = res[0][2] / res[0][1]
    o1 = res[1][2] / res[1][1]
    lane = lax.broadcasted_iota(jnp.int32, o0.shape, 1)
    o_ref[...] = jnp.where(lane < V_DIM, o0, o1)


def _attn(q, k, v, tq, tk):
    B, L, _ = q.shape
    return pl.pallas_call(
        functools.partial(_attn_body, tk=tk),
        grid=(B, N_HEADS // 2, L // tq),
        in_specs=[pl.BlockSpec((None, tq, 2 * HEAD_PAD), lambda b, h, i: (b, i, h)),
                  pl.BlockSpec((None, L, 2 * HEAD_PAD), lambda b, h, i: (b, 0, h)),
                  pl.BlockSpec((None, L, 2 * V_DIM), lambda b, h, i: (b, 0, h))],
        out_specs=pl.BlockSpec((None, tq, 2 * V_DIM), lambda b, h, i: (b, i, h)),
        out_shape=jax.ShapeDtypeStruct((B, L, N_HEADS * V_DIM), F32),
        compiler_params=_params("parallel", "parallel", "parallel"),
        name="attn",
    )(q, k, v)


def _tail_body(x_ref, yh_ref, om_ref, p_ref, hnw_ref, mnw_ref, wo_ref, mlw_ref, w1_ref, w2_ref, wg_ref,
               wp_ref, pnw_ref, o_ref, *, fc):
    dot = functools.partial(jnp.dot, preferred_element_type=F32)
    a = _rms(yh_ref[...], hnw_ref[...]).astype(BF16)
    b = _rms(om_ref[...], mnw_ref[...]).astype(BF16)
    x = x_ref[...] + dot(a, wo_ref[:HY_WIDTH, :]) + dot(b, wo_ref[HY_WIDTH:, :])
    xn = _rms(x, mlw_ref[...]).astype(BF16)
    mlp = jnp.zeros_like(x)
    for c in range(w1_ref.shape[1] // fc):
        h = jnp.maximum(dot(xn, w1_ref[:, c * fc:(c + 1) * fc]), 0.0)
        mlp = mlp + dot((h * h).astype(BF16), w2_ref[c * fc:(c + 1) * fc, :])
    x = x + mlp
    gate = jax.nn.sigmoid(dot(x.astype(BF16), wg_ref[...]))
    pp = _rms(dot(p_ref[...].astype(BF16), wp_ref[...]), pnw_ref[...])
    o_ref[...] = x + gate * pp


def _tail(x, yh, om, p, hnw, mnw, wo, mlw, w1, w2, wg, wp, pnw, tm, fc):
    B, L, D = x.shape
    tok = lambda w: pl.BlockSpec((None, tm, w), lambda b, t: (b, t, 0))
    res = lambda a: pl.BlockSpec(a.shape, lambda b, t: (0,) * a.ndim, pipeline_mode=pl.Buffered(1))
    return pl.pallas_call(
        functools.partial(_tail_body, fc=fc),
        grid=(B, L // tm),
        in_specs=[tok(D), tok(yh.shape[-1]), tok(om.shape[-1]), tok(p.shape[-1]),
                  res(hnw), res(mnw), res(wo), res(mlw), res(w1), res(w2), res(wg), res(wp), res(pnw)],
        out_specs=tok(D),
        out_shape=jax.ShapeDtypeStruct((B, L, D), F32),
        compiler_params=_params("parallel", "parallel"),
        name="tail",
    )(x, yh, om, p, hnw, mnw, wo, mlw, w1, w2, wg, wp, pnw)


def _rope_tables(L):
    inv = ROPE_THETA ** (-(2.0 * jnp.arange(HALF_ROPE, dtype=F32)) / QK_ROPE)
    ang = jnp.arange(L, dtype=F32)[:, None] * inv[None, :]
    cos, sin = jnp.cos(ang), jnp.sin(ang)
    one = jnp.ones((L, QK_NOPE), F32)
    pad = jnp.zeros((L, HEAD_PAD - QK_DIM), F32)
    cosf = jnp.concatenate([one, cos, cos, pad], axis=1)
    sinf = jnp.concatenate([0.0 * one, -sin, sin, pad], axis=1)
    return cosf, sinf


def _filter_features(L):
    t = jnp.linspace(0.0, 1.0, L, dtype=F32)[:, None]
    w = (2.0 * math.pi) * jnp.arange(L, dtype=F32)[:, None] / L
    f = jnp.linspace(1e-4, FILT_BANDS - 1, FILT_BANDS, dtype=F32)[None, :]
    z = jnp.concatenate([t, jnp.cos(f * w), -jnp.sin(f * w)], axis=-1)
    z = jnp.pad(z, ((0, 0), (0, LANES - z.shape[1])))
    return jnp.concatenate([z, z[::-1]], axis=0)


def _decay_rates():
    min_decay = math.log(DECAY_TARGET) / FAST_DECAY_PCT
    max_decay = math.log(DECAY_TARGET) / SLOW_DECAY_PCT
    return jnp.abs(jnp.linspace(min_decay, max_decay, HY_WIDTH, dtype=F32))[None, :]


def _pad_heads(w, width):
    k = w.shape[0]
    w = w.reshape(k, N_HEADS, width)
    return jnp.pad(w, ((0, 0), (0, 0), (0, HEAD_PAD - width))).reshape(k, N_HEADS * HEAD_PAD)


def _layer_weights(attn_norm_w, w_in, q_a_norm_w, w_q_b, kv_a_norm_w, w_kv_b, q_norm_w, k_norm_w):
    d = w_in.shape[0]
    o3 = HY3 + Q_LORA + KV_LORA
    win = jnp.concatenate([w_in[:, :o3], jnp.zeros((d, QK_NOPE), F32), w_in[:, o3:],
                           jnp.zeros((d, HEAD_PAD - QK_DIM), F32)], axis=1).astype(BF16)
    wqb = _pad_heads(w_q_b, QK_DIM).astype(BF16)
    kv = w_kv_b.reshape(KV_LORA, N_HEADS, QK_NOPE + V_DIM)
    wk = _pad_heads(kv[:, :, :QK_NOPE].reshape(KV_LORA, N_HEADS * QK_NOPE), QK_NOPE)
    wv = kv[:, :, QK_NOPE:].reshape(KV_LORA, N_HEADS * V_DIM)
    wkvb = jnp.concatenate([wk, wv], axis=1).astype(BF16)
    padw = lambda w: jnp.pad(w, (0, HEAD_PAD - QK_DIM))[None, :]
    return (attn_norm_w[None, :], win, q_a_norm_w[None, :], wqb, kv_a_norm_w[None, :], wkvb,
            padw(q_norm_w), padw(k_norm_w))


def _group(x, p, lw, conv_w, conv_b, filt, filt_bias, tailw):
    B, L, D = x.shape
    cosf, sinf = _rope_tables(L)
    hy, q, k, v = _inproj(x, *lw, cosf, sinf, tm=512)
    x0, z = _gate(hy, conv_w, conv_b, tl=512)
    kern = _filt(_filter_features(L), *filt, _decay_rates(), tf=512)
    yh = _hyena_conv(x0, z, kern, filt_bias, ct=2048, cw=HY_WIDTH)
    om = _attn(q, k, v, tq=512, tk=512)
    return _tail(x, yh, om, p, *tailw, tm=512, fc=1024)


def kernel(x_prompt, x_sample, p_prompt, p_sample, attn_norm_w, w_in, conv_w, conv_b, filt_w1, filt_b1, filt_freq, filt_w2, filt_b2, filt_w3, filt_bias, q_a_norm_w, w_q_b, kv_a_norm_w, w_kv_b, q_norm_w, k_norm_w, hy_out_norm_w, mla_out_norm_w, w_out, mlp_norm_w, w_mlp1, w_mlp2, w_ple_gate, w_ple_proj, ple_norm_w):
    depth = w_in.shape[0]
    for i in range(depth):
        lw = _layer_weights(attn_norm_w[i], w_in[i], q_a_norm_w[i], w_q_b[i], kv_a_norm_w[i], w_kv_b[i],
                            q_norm_w[i], k_norm_w[i])
        w1 = jnp.pad(filt_w1[i], ((0, LANES - filt_w1.shape[1]), (0, 0)))
        filt = (w1, filt_b1[i][None, :], filt_freq[i][None, :], filt_w2[i], filt_b2[i][None, :], filt_w3[i])
        tailw = (hy_out_norm_w[i][None, :], mla_out_norm_w[i][None, :], w_out[i].astype(BF16),
                 mlp_norm_w[i][None, :], w_mlp1[i].astype(BF16), w_mlp2[i].astype(BF16),
                 w_ple_gate[i].astype(BF16), w_ple_proj[i].astype(BF16), ple_norm_w[i][None, :])
        x_prompt = _group(x_prompt, p_prompt[i], lw, conv_w[i], conv_b[i][None, :], filt, filt_bias[i], tailw)
        x_sample = _group(x_sample, p_sample[i], lw, conv_w[i], conv_b[i][None, :], filt, filt_bias[i], tailw)
    return (x_prompt, x_sample)
```

```python
import functools
import math

import jax
import jax.numpy as jnp
from jax import lax
from jax.experimental import pallas as pl
from jax.experimental.pallas import tpu as pltpu

F32 = jnp.float32
BF16 = jnp.bfloat16
HIGHEST = lax.Precision.HIGHEST

LANES = 128
SUBLANES = 8
VMEM_LIMIT = 56 * 1024 * 1024

HY_WIDTH = 512
HY_STREAMS = 3
HY3 = HY_STREAMS * HY_WIDTH
FILT_BANDS = 16
FILT_HIDDEN = 64
DECAY_TARGET = 1e-2
FAST_DECAY_PCT = 0.3
SLOW_DECAY_PCT = 1.5
N_HEADS = 8
QK_NOPE = 64
QK_ROPE = 32
QK_DIM = QK_NOPE + QK_ROPE
V_DIM = 64
Q_LORA = 256
KV_LORA = 128
ROPE_THETA = 10000.0
EPS = 1e-6
HEAD_PAD = LANES
HALF_ROPE = QK_ROPE // 2


def _rms(x, w):
    ms = jnp.mean(x * x, axis=-1, keepdims=True)
    return x * lax.rsqrt(ms + EPS) * w


def _params(*sem):
    return pltpu.CompilerParams(dimension_semantics=sem, vmem_limit_bytes=VMEM_LIMIT)


def _const_spec(shape):
    nd = len(shape)
    return pl.BlockSpec(shape, lambda *_: (0,) * nd)


def _inproj_body(x_ref, anw_ref, win_ref, qaw_ref, wqb_ref, kvaw_ref, wkvb_ref, qnw_ref, knw_ref,
                 cos_ref, sin_ref, hy_ref, q_ref, k_ref, v_ref):
    xn = _rms(x_ref[...], anw_ref[...])
    proj = jnp.dot(xn.astype(BF16), win_ref[...], preferred_element_type=F32)
    hy_ref[...] = proj[:, :HY3]
    o2 = HY3 + Q_LORA
    o3 = o2 + KV_LORA
    o4 = o3 + HEAD_PAD
    q_lat, kv_lat, krope, krope_pi = proj[:, HY3:o2], proj[:, o2:o3], proj[:, o3:o4], proj[:, o4:]
    qf = jnp.dot(_rms(q_lat, qaw_ref[...]).astype(BF16), wqb_ref[...], preferred_element_type=F32)
    kvf = jnp.dot(_rms(kv_lat, kvaw_ref[...]).astype(BF16), wkvb_ref[...], preferred_element_type=F32)
    kw = N_HEADS * HEAD_PAD
    lane = lax.broadcasted_iota(jnp.int32, (1, kw), 1) % (2 * HEAD_PAD)
    ones_col = (lane == V_DIM) | (lane == HEAD_PAD)
    v_ref[...] = jnp.where(ones_col, 1.0, kvf[:, kw:]).astype(BF16)

    cosf, sinf = cos_ref[...], sin_ref[...]
    qscale = QK_DIM ** -0.5 * math.log2(math.e)
    cq, sq = cosf * (qnw_ref[0:1] * qscale), sinf * (qnw_ref[1:2] * qscale)
    ck, sk = cosf * knw_ref[0:1], sinf * knw_ref[1:2]
    k_pi = krope_pi * sk

    def rstd(xh):
        return lax.rsqrt(jnp.sum(xh * xh, axis=-1, keepdims=True) / QK_DIM + EPS)

    for h in range(N_HEADS):
        sl = slice(h * HEAD_PAD, (h + 1) * HEAD_PAD)
        xq = qf[:, sl]
        q_ref[:, sl] = ((xq * cq + qf[:, kw + h * HEAD_PAD:kw + (h + 1) * HEAD_PAD] * sq) * rstd(xq)).astype(BF16)
        xk = kvf[:, sl] + krope
        k_ref[:, sl] = ((xk * ck + k_pi) * rstd(xk)).astype(BF16)


def _inproj(x, anw, win, qaw, wqb, kvaw, wkvb, qnw, knw, cosf, sinf, tm):
    B, L, D = x.shape
    kw = N_HEADS * HEAD_PAD
    tok = lambda w: pl.BlockSpec((None, tm, w), lambda b, t: (b, t, 0))
    pos = pl.BlockSpec((tm, HEAD_PAD), lambda b, t: (t, 0))
    return pl.pallas_call(
        _inproj_body,
        grid=(B, L // tm),
        in_specs=[tok(D), _const_spec(anw.shape), _const_spec(win.shape), _const_spec(qaw.shape),
                  _const_spec(wqb.shape), _const_spec(kvaw.shape), _const_spec(wkvb.shape),
                  _const_spec(qnw.shape), _const_spec(knw.shape), pos, pos],
        out_specs=[tok(HY3), tok(kw), tok(kw), tok(kw)],
        out_shape=[jax.ShapeDtypeStruct((B, L, HY3), F32)] + [jax.ShapeDtypeStruct((B, L, kw), BF16)] * 3,
        compiler_params=_params("parallel", "parallel"),
        name="inproj",
    )(x, anw, win, qaw, wqb, kvaw, wkvb, qnw, knw, cosf, sinf)


def _gate_body(prev_ref, cur_ref, next_ref, cw_ref, cb_ref, x0_ref, z_ref):
    t = pl.program_id(1)
    nt = pl.num_programs(1)
    tl = cur_ref.shape[0]
    row = lax.broadcasted_iota(jnp.int32, (tl, HY_WIDTH), 0)
    has_prev = (t > 0).astype(F32)
    has_next = (t < nt - 1).astype(F32)

    def stream(s):
        sl = slice(s * HY_WIDTH, (s + 1) * HY_WIDTH)
        cur = cur_ref[:, sl]
        prow = prev_ref[SUBLANES - 1:SUBLANES, sl] * has_prev
        nrow = next_ref[0:1, sl] * has_next
        dn = jnp.where(row == 0, prow, pltpu.roll(cur, 1, 0))
        up = jnp.where(row == tl - 1, nrow, pltpu.roll(cur, tl - 1, 0))
        return cb_ref[:, sl] + dn * cw_ref[0:1, sl] + cur * cw_ref[1:2, sl] + up * cw_ref[2:3, sl]

    x0_ref[...] = stream(0)
    z_ref[...] = stream(2) * stream(1)


def _gate(hy, conv_w, conv_b, tl):
    B, L, _ = hy.shape
    nb = tl // SUBLANES
    last = L // SUBLANES - 1
    out = pl.BlockSpec((None, tl, HY_WIDTH), lambda b, t: (b, t, 0))
    return pl.pallas_call(
        _gate_body,
        grid=(B, L // tl),
        in_specs=[pl.BlockSpec((None, SUBLANES, HY3), lambda b, t: (b, jnp.maximum(t * nb - 1, 0), 0)),
                  pl.BlockSpec((None, tl, HY3), lambda b, t: (b, t, 0)),
                  pl.BlockSpec((None, SUBLANES, HY3), lambda b, t: (b, jnp.minimum((t + 1) * nb, last), 0)),
                  _const_spec(conv_w.shape), _const_spec(conv_b.shape)],
        out_specs=[out, out],
        out_shape=[jax.ShapeDtypeStruct((B, L, HY_WIDTH), F32)] * 2,
        compiler_params=_params("parallel", "parallel"),
        name="gate",
    )(hy, hy, hy, conv_w, conv_b)


def _filt_body(z_ref, w1_ref, b1_ref, fr_ref, w2_ref, b2_ref, w3_ref, dl_ref, o_ref):
    z = z_ref[...]
    fr = fr_ref[...]
    dot = functools.partial(jnp.dot, precision=HIGHEST, preferred_element_type=F32)
    h = jnp.sin(fr * (dot(z, w1_ref[...]) + b1_ref[...]))
    h = jnp.sin(fr * (dot(h, w2_ref[...]) + b2_ref[...]))
    h = dot(h, w3_ref[...])
    o_ref[...] = h * jnp.exp(-z[:, 0:1] * dl_ref[...])


def _filt(feat, w1, b1, fr, w2, b2, w3, deltas, tf):
    n2l = feat.shape[0]
    half = n2l // tf // 2
    return pl.pallas_call(
        _filt_body,
        grid=(n2l // tf,),
        in_specs=[pl.BlockSpec((tf, LANES), lambda t: (t, 0)), _const_spec(w1.shape), _const_spec(b1.shape),
                  _const_spec(fr.shape), _const_spec(w2.shape), _const_spec(b2.shape),
                  pl.BlockSpec((FILT_HIDDEN, HY_WIDTH), lambda t: (0, t // half)),
                  _const_spec(deltas.shape)],
        out_specs=pl.BlockSpec((tf, HY_WIDTH), lambda t: (t, 0)),
        out_shape=jax.ShapeDtypeStruct((n2l, HY_WIDTH), F32),
        compiler_params=_params("parallel"),
        name="filt",
    )(feat, w1, b1, fr, w2, b2, w3, deltas)


def _dft1_body(c_ref, x_ref, o_ref):
    o_ref[...] = jnp.dot(c_ref[...], x_ref[...], precision=HIGHEST, preferred_element_type=F32)


def _dft1(cmat, xv, ct):
    P, K, W = xv.shape
    M = cmat.shape[0]
    return pl.pallas_call(
        _dft1_body,
        grid=(P, W // ct),
        in_specs=[_const_spec(cmat.shape), pl.BlockSpec((None, K, ct), lambda p, c: (p, 0, c))],
        out_specs=pl.BlockSpec((None, M, ct), lambda p, c: (p, 0, c)),
        out_shape=jax.ShapeDtypeStruct((P, M, W), F32),
        compiler_params=_params("parallel", "parallel"),
        name="dft1",
    )(cmat, xv)


def _mid_kernel_body(m_ref, a_ref, o_ref):
    n2 = a_ref.shape[1]
    a = a_ref[...].reshape(2 * n2, a_ref.shape[2])
    x = jnp.dot(m_ref[...], a, precision=HIGHEST, preferred_element_type=F32)
    o_ref[...] = x.reshape(o_ref.shape)


def _mid_body(m_ref, mt_ref, kf_ref, a_ref, o_ref):
    n2 = a_ref.shape[1]
    a = a_ref[...].reshape(2 * n2, a_ref.shape[2])
    x = jnp.dot(m_ref[...], a, precision=HIGHEST, preferred_element_type=F32)
    xr, xi = x[:n2], x[n2:]
    kr, ki = kf_ref[0], kf_ref[1]
    y = jnp.concatenate([xr * kr - xi * ki, xr * ki + xi * kr], axis=0)
    b = jnp.dot(mt_ref[...], y, precision=HIGHEST, preferred_element_type=F32)
    o_ref[...] = b.reshape(o_ref.shape)


def _mid(mblk, mblk_t, kf, a, cw):
    P, _, n1, n2, C = a.shape
    dat = pl.BlockSpec((None, 2, None, n2, cw), lambda k, c, p: (p, 0, k, 0, c))
    mat = pl.BlockSpec((None, 2 * n2, 2 * n2), lambda k, c, p: (k, 0, 0))
    if kf is None:
        body, ins, specs = _mid_kernel_body, (mblk, a), [mat, dat]
    else:
        kspec = pl.BlockSpec((None, 2, None, n2, cw), lambda k, c, p: (0, 0, k, 0, c))
        body, ins, specs = _mid_body, (mblk, mblk_t, kf, a), [mat, mat, kspec, dat]
    return pl.pallas_call(
        body,
        grid=(n1, C // cw, P),
        in_specs=specs,
        out_specs=dat,
        out_shape=jax.ShapeDtypeStruct(a.shape, F32),
        compiler_params=_params("parallel", "parallel", "parallel"),
        name="mid",
    )(*ins)


def _dft4_body(c_ref, b_ref, x0_ref, z_ref, bias_ref, o_ref):
    y = jnp.dot(c_ref[...], b_ref[...], precision=HIGHEST, preferred_element_type=F32)
    o_ref[...] = x0_ref[...] * (y + z_ref[...] * bias_ref[...])


def _dft4(cmat, bv, x0v, zv, bias_t, ct):
    P, K, W = bv.shape
    M = cmat.shape[0]
    row = pl.BlockSpec((None, M, ct), lambda p, c: (p, 0, c))
    return pl.pallas_call(
        _dft4_body,
        grid=(P, W // ct),
        in_specs=[_const_spec(cmat.shape), pl.BlockSpec((None, K, ct), lambda p, c: (p, 0, c)), row, row,
                  _const_spec(bias_t.shape)],
        out_specs=row,
        out_shape=jax.ShapeDtypeStruct((P, M, W), F32),
        compiler_params=_params("parallel", "parallel"),
        name="dft4",
    )(cmat, bv, x0v, zv, bias_t)


def _dft_tables(n1, n2):
    n = n1 * n2
    h = n1 // 2
    i1 = jnp.arange(n1, dtype=jnp.int32)
    th = (2.0 * math.pi / n1) * ((i1[:, None] * i1[None, :]) % n1).astype(F32)
    c1, s1 = jnp.cos(th), jnp.sin(th)
    cz = jnp.block([[c1[:, :h], s1[:, :h]], [-s1[:, :h], c1[:, :h]]])
    ck = jnp.concatenate([c1, -s1], axis=0)
    c4 = jnp.block([[c1[:h], -s1[:h]], [s1[:h], c1[:h]]]) * (1.0 / n)
    k1 = jnp.arange(n1, dtype=jnp.int32)[:, None, None]
    k2 = jnp.arange(n2, dtype=jnp.int32)[None, :, None]
    i2 = jnp.arange(n2, dtype=jnp.int32)[None, None, :]
    ph = (2.0 * math.pi / n) * ((i2 * (k1 + n1 * k2)) % n).astype(F32)
    mr, mi = jnp.cos(ph), -jnp.sin(ph)
    mblk = jnp.concatenate([jnp.concatenate([mr, -mi], axis=2), jnp.concatenate([mi, mr], axis=2)], axis=1)
    return cz, ck, c4, mblk, jnp.swapaxes(mblk, 1, 2)


def _split(L):
    n = 2 * L
    n1 = 1 << ((n.bit_length() - 1) // 2)
    return n1, n // n1


def _hyena_conv(x0, z, kern, filt_bias, ct, cw):
    B, L, C = z.shape
    n1, n2 = _split(L)
    P = B // 2
    cz, ck, c4, mblk, mblk_t = _dft_tables(n1, n2)
    w = n2 * C
    ka = _dft1(ck, kern.reshape(1, n1, w), ct)
    kf = _mid(mblk, None, None, ka.reshape(1, 2, n1, n2, C), cw)
    za = _dft1(cz, z.reshape(P, n1, w), ct)
    zb = _mid(mblk, mblk_t, kf, za.reshape(P, 2, n1, n2, C), cw)
    bias_t = jnp.tile(filt_bias.reshape(1, C), (1, ct // C))
    y = _dft4(c4, zb.reshape(P, 2 * n1, w), x0.reshape(P, n1, w), z.reshape(P, n1, w), bias_t, ct)
    return y.reshape(B, L, C)


def _attn_body(q_ref, k_ref, v_ref, o_ref, s_scr, p_scr, m_scr, acc_scr, *, tk, rb):
    tq = q_ref.shape[0]
    nk = k_ref.shape[0] // tk
    nc = tk // LANES
    m_scr[...] = jnp.full(m_scr.shape, -jnp.inf, F32)
    acc_scr[...] = jnp.zeros(acc_scr.shape, F32)

    def step(j, carry):
        off = pl.multiple_of(j * tk, tk)
        for h in range(2):
            hs = slice(h * HEAD_PAD, (h + 1) * HEAD_PAD)
            s_scr[h] = lax.dot_general(q_ref[:, hs], k_ref[pl.ds(off, tk), hs], (((1,), (1,)), ((), ())),
                                       preferred_element_type=F32)
        for h in range(2):
            hs = slice(h * HEAD_PAD, (h + 1) * HEAD_PAD)
            for r in range(tq // rb):
                rows = slice(r * rb, (r + 1) * rb)
                mx = s_scr[h, rows, 0:LANES]
                for c in range(1, nc):
                    mx = jnp.maximum(mx, s_scr[h, rows, c * LANES:(c + 1) * LANES])
                m_old = m_scr[h, rows, :]
                m_new = jnp.maximum(m_old, jnp.max(mx, axis=-1, keepdims=True))
                m_scr[h, rows, :] = m_new
                acc_scr[h, rows, :] = acc_scr[h, rows, :] * jnp.exp2(m_old - m_new)
                for c in range(nc):
                    cs = slice(c * LANES, (c + 1) * LANES)
                    p_scr[h, rows, cs] = jnp.exp2(s_scr[h, rows, cs] - m_new).astype(BF16)
            acc_scr[h] += jnp.dot(p_scr[h], v_ref[pl.ds(off, tk), hs], preferred_element_type=F32)
        return carry

    lax.fori_loop(0, nk, step, 0)
    a0, a1 = acc_scr[0], acc_scr[1]
    lane = lax.broadcasted_iota(jnp.int32, a0.shape, 1)
    o_ref[...] = jnp.where(lane < V_DIM, a0 / a0[:, V_DIM:V_DIM + 1], a1 / a1[:, 0:1])


def _attn(q, k, v, tq, tk, rb):
    B, L, _ = q.shape
    pair = 2 * HEAD_PAD
    return pl.pallas_call(
        functools.partial(_attn_body, tk=tk, rb=rb),
        grid=(B, N_HEADS // 2, L // tq),
        in_specs=[pl.BlockSpec((None, tq, pair), lambda b, h, i: (b, i, h)),
                  pl.BlockSpec((None, L, pair), lambda b, h, i: (b, 0, h)),
                  pl.BlockSpec((None, L, pair), lambda b, h, i: (b, 0, h))],
        out_specs=pl.BlockSpec((None, tq, 2 * V_DIM), lambda b, h, i: (b, i, h)),
        out_shape=jax.ShapeDtypeStruct((B, L, N_HEADS * V_DIM), F32),
        scratch_shapes=[pltpu.VMEM((2, tq, tk), F32), pltpu.VMEM((2, tq, tk), BF16),
                        pltpu.VMEM((2, tq, HEAD_PAD), F32), pltpu.VMEM((2, tq, HEAD_PAD), F32)],
        compiler_params=_params("parallel", "parallel", "parallel"),
        name="attn",
    )(q, k, v)


def _tail_body(x_ref, yh_ref, om_ref, p_ref, hnw_ref, mnw_ref, wo_ref, mlw_ref, w1_ref, w2_ref, wg_ref,
               wp_ref, pnw_ref, o_ref, *, fc):
    dot = functools.partial(jnp.dot, preferred_element_type=F32)
    a = _rms(yh_ref[...], hnw_ref[...]).astype(BF16)
    b = _rms(om_ref[...], mnw_ref[...]).astype(BF16)
    x = x_ref[...] + dot(a, wo_ref[:HY_WIDTH, :]) + dot(b, wo_ref[HY_WIDTH:, :])
    xn = _rms(x, mlw_ref[...]).astype(BF16)
    mlp = jnp.zeros_like(x)
    for c in range(w1_ref.shape[1] // fc):
        h = jnp.maximum(dot(xn, w1_ref[:, c * fc:(c + 1) * fc]), 0.0)
        mlp = mlp + dot((h * h).astype(BF16), w2_ref[c * fc:(c + 1) * fc, :])
    x = x + mlp
    gate = jax.nn.sigmoid(dot(x.astype(BF16), wg_ref[...]))
    pp = _rms(dot(p_ref[...].astype(BF16), wp_ref[...]), pnw_ref[...])
    o_ref[...] = x + gate * pp


def _tail(x, yh, om, p, hnw, mnw, wo, mlw, w1, w2, wg, wp, pnw, tm, fc):
    B, L, D = x.shape
    tok = lambda w: pl.BlockSpec((None, tm, w), lambda b, t: (b, t, 0))
    res = lambda a: pl.BlockSpec(a.shape, lambda b, t: (0,) * a.ndim, pipeline_mode=pl.Buffered(1))
    return pl.pallas_call(
        functools.partial(_tail_body, fc=fc),
        grid=(B, L // tm),
        in_specs=[tok(D), tok(yh.shape[-1]), tok(om.shape[-1]), tok(p.shape[-1]),
                  res(hnw), res(mnw), res(wo), res(mlw), res(w1), res(w2), res(wg), res(wp), res(pnw)],
        out_specs=tok(D),
        out_shape=jax.ShapeDtypeStruct((B, L, D), F32),
        compiler_params=_params("parallel", "parallel"),
        name="tail",
    )(x, yh, om, p, hnw, mnw, wo, mlw, w1, w2, wg, wp, pnw)


def _rope_tables(L):
    inv = ROPE_THETA ** (-(2.0 * jnp.arange(HALF_ROPE, dtype=F32)) / QK_ROPE)
    ang = jnp.arange(L, dtype=F32)[:, None] * inv[None, :]
    cos, sin = jnp.cos(ang), jnp.sin(ang)
    one = jnp.ones((L, QK_NOPE), F32)
    pad = jnp.zeros((L, HEAD_PAD - QK_DIM), F32)
    cosf = jnp.concatenate([one, cos, cos, pad], axis=1)
    sinf = jnp.concatenate([0.0 * one, -sin, sin, pad], axis=1)
    return cosf, sinf


def _filter_features(L):
    t = jnp.linspace(0.0, 1.0, L, dtype=F32)[:, None]
    w = (2.0 * math.pi) * jnp.arange(L, dtype=F32)[:, None] / L
    f = jnp.linspace(1e-4, FILT_BANDS - 1, FILT_BANDS, dtype=F32)[None, :]
    z = jnp.concatenate([t, jnp.cos(f * w), -jnp.sin(f * w)], axis=-1)
    z = jnp.pad(z, ((0, 0), (0, LANES - z.shape[1])))
    return jnp.concatenate([z, z[::-1]], axis=0)


def _decay_rates():
    min_decay = math.log(DECAY_TARGET) / FAST_DECAY_PCT
    max_decay = math.log(DECAY_TARGET) / SLOW_DECAY_PCT
    return jnp.abs(jnp.linspace(min_decay, max_decay, HY_WIDTH, dtype=F32))[None, :]


def _pad_heads(w, width, lead=0):
    w = w.reshape(w.shape[:-1] + (N_HEADS, width))
    pad = [(0, 0)] * (w.ndim - 1) + [(lead, HEAD_PAD - width - lead)]
    return jnp.pad(w, pad).reshape(w.shape[:-2] + (N_HEADS * HEAD_PAD,))


def _rope_partner(w):
    a, b = w[..., QK_NOPE:QK_NOPE + HALF_ROPE], w[..., QK_NOPE + HALF_ROPE:]
    zeros = lambda n: jnp.zeros(w.shape[:-1] + (n,), w.dtype)
    return jnp.concatenate([zeros(QK_NOPE), b, a, zeros(HEAD_PAD - QK_DIM)], axis=-1)


def _layer_weights(attn_norm_w, w_in, q_a_norm_w, w_q_b, kv_a_norm_w, w_kv_b, q_norm_w, k_norm_w):
    d = w_in.shape[0]
    o3 = HY3 + Q_LORA + KV_LORA
    rope_cols = jnp.pad(w_in[:, o3:], ((0, 0), (QK_NOPE, 0)))
    win = jnp.concatenate([w_in[:, :o3], jnp.pad(rope_cols, ((0, 0), (0, HEAD_PAD - QK_DIM))),
                           _rope_partner(rope_cols)], axis=1).astype(BF16)
    wq = w_q_b.reshape(Q_LORA, N_HEADS, QK_DIM)
    wqb = jnp.concatenate([_pad_heads(w_q_b, QK_DIM), _rope_partner(wq).reshape(Q_LORA, N_HEADS * HEAD_PAD)],
                          axis=1).astype(BF16)
    kv = w_kv_b.reshape(KV_LORA, N_HEADS, QK_NOPE + V_DIM)
    wk = _pad_heads(kv[:, :, :QK_NOPE].reshape(KV_LORA, N_HEADS * QK_NOPE), QK_NOPE)
    v4 = kv[:, :, QK_NOPE:].reshape(KV_LORA, N_HEADS // 2, 2, V_DIM)
    zero = jnp.zeros_like(v4[:, :, 0])
    wv = jnp.stack([v4[:, :, 0], zero, zero, v4[:, :, 1]], axis=2).reshape(KV_LORA, N_HEADS * HEAD_PAD)
    wkvb = jnp.concatenate([wk, wv], axis=1).astype(BF16)
    norm2 = lambda w: jnp.stack([jnp.pad(w, (0, HEAD_PAD - QK_DIM)), _rope_partner(w)])
    return (attn_norm_w[None, :], win, q_a_norm_w[None, :], wqb, kv_a_norm_w[None, :], wkvb,
            norm2(q_norm_w), norm2(k_norm_w))


def _group(x, p, lw, conv_w, conv_b, filt, filt_bias, tailw):
    B, L, D = x.shape
    cosf, sinf = _rope_tables(L)
    hy, q, k, v = _inproj(x, *lw, cosf, sinf, tm=512)
    x0, z = _gate(hy, conv_w, conv_b, tl=512)
    kern = _filt(_filter_features(L), *filt, _decay_rates(), tf=512)
    yh = _hyena_conv(x0, z, kern, filt_bias, ct=2048, cw=HY_WIDTH)
    om = _attn(q, k, v, tq=512, tk=min(1024, L), rb=128)
    return _tail(x, yh, om, p, *tailw, tm=512, fc=1024)


def kernel(x_prompt, x_sample, p_prompt, p_sample, attn_norm_w, w_in, conv_w, conv_b, filt_w1, filt_b1, filt_freq, filt_w2, filt_b2, filt_w3, filt_bias, q_a_norm_w, w_q_b, kv_a_norm_w, w_kv_b, q_norm_w, k_norm_w, hy_out_norm_w, mla_out_norm_w, w_out, mlp_norm_w, w_mlp1, w_mlp2, w_ple_gate, w_ple_proj, ple_norm_w):
    depth = w_in.shape[0]
    for i in range(depth):
        lw = _layer_weights(attn_norm_w[i], w_in[i], q_a_norm_w[i], w_q_b[i], kv_a_norm_w[i], w_kv_b[i],
                            q_norm_w[i], k_norm_w[i])
        w1 = jnp.pad(filt_w1[i], ((0, LANES - filt_w1.shape[1]), (0, 0)))
        filt = (w1, filt_b1[i][None, :], filt_freq[i][None, :], filt_w2[i], filt_b2[i][None, :], filt_w3[i])
        tailw = (hy_out_norm_w[i][None, :], mla_out_norm_w[i][None, :], w_out[i].astype(BF16),
                 mlp_norm_w[i][None, :], w_mlp1[i].astype(BF16), w_mlp2[i].astype(BF16),
                 w_ple_gate[i].astype(BF16), w_ple_proj[i].astype(BF16), ple_norm_w[i][None, :])
        x_prompt = _group(x_prompt, p_prompt[i], lw, conv_w[i], conv_b[i][None, :], filt, filt_bias[i], tailw)
        x_sample = _group(x_sample, p_sample[i], lw, conv_w[i], conv_b[i][None, :], filt, filt_bias[i], tailw)
    return (x_prompt, x_sample)
```

```python
import functools
import math

import jax
import jax.numpy as jnp
from jax import lax
from jax.experimental import pallas as pl
from jax.experimental.pallas import tpu as pltpu

F32 = jnp.float32
BF16 = jnp.bfloat16
HIGHEST = lax.Precision.HIGHEST

LANES = 128
SUBLANES = 8
VMEM_LIMIT = 56 * 1024 * 1024

HY_WIDTH = 512
HY_STREAMS = 3
HY3 = HY_STREAMS * HY_WIDTH
FILT_BANDS = 16
FILT_HIDDEN = 64
DECAY_TARGET = 1e-2
FAST_DECAY_PCT = 0.3
SLOW_DECAY_PCT = 1.5
N_HEADS = 8
QK_NOPE = 64
QK_ROPE = 32
QK_DIM = QK_NOPE + QK_ROPE
V_DIM = 64
Q_LORA = 256
KV_LORA = 128
ROPE_THETA = 10000.0
EPS = 1e-6
HEAD_PAD = LANES
HALF_ROPE = QK_ROPE // 2


def _rms(x, w):
    ms = jnp.mean(x * x, axis=-1, keepdims=True)
    return x * lax.rsqrt(ms + EPS) * w


def _params(*sem):
    return pltpu.CompilerParams(dimension_semantics=sem, vmem_limit_bytes=VMEM_LIMIT)


def _const_spec(shape):
    nd = len(shape)
    return pl.BlockSpec(shape, lambda *_: (0,) * nd)


def _inproj_body(x_ref, anw_ref, win_ref, qaw_ref, wqb_ref, kvaw_ref, wkvb_ref, qnw_ref, knw_ref,
                 cos_ref, sin_ref, hy_ref, q_ref, k_ref, v_ref):
    xn = _rms(x_ref[...], anw_ref[...])
    proj = jnp.dot(xn.astype(BF16), win_ref[...], preferred_element_type=F32)
    hy_ref[...] = proj[:, :HY3]
    o2 = HY3 + Q_LORA
    o3 = o2 + KV_LORA
    o4 = o3 + HEAD_PAD
    q_lat, kv_lat, krope, krope_pi = proj[:, HY3:o2], proj[:, o2:o3], proj[:, o3:o4], proj[:, o4:]
    qf = jnp.dot(_rms(q_lat, qaw_ref[...]).astype(BF16), wqb_ref[...], preferred_element_type=F32)
    kvf = jnp.dot(_rms(kv_lat, kvaw_ref[...]).astype(BF16), wkvb_ref[...], preferred_element_type=F32)
    kw = N_HEADS * HEAD_PAD
    lane = lax.broadcasted_iota(jnp.int32, (1, kw), 1) % (2 * HEAD_PAD)
    ones_col = (lane == V_DIM) | (lane == HEAD_PAD)
    v_ref[...] = jnp.where(ones_col, 1.0, kvf[:, kw:]).astype(BF16)

    cosf, sinf = cos_ref[...], sin_ref[...]
    qscale = QK_DIM ** -0.5 * math.log2(math.e)
    cq, sq = cosf * (qnw_ref[0:1] * qscale), sinf * (qnw_ref[1:2] * qscale)
    ck, sk = cosf * knw_ref[0:1], sinf * knw_ref[1:2]
    k_pi = krope_pi * sk

    def rstd(xh):
        return lax.rsqrt(jnp.sum(xh * xh, axis=-1, keepdims=True) / QK_DIM + EPS)

    for h in range(N_HEADS):
        sl = slice(h * HEAD_PAD, (h + 1) * HEAD_PAD)
        xq = qf[:, sl]
        q_ref[:, sl] = ((xq * cq + qf[:, kw + h * HEAD_PAD:kw + (h + 1) * HEAD_PAD] * sq) * rstd(xq)).astype(BF16)
        xk = kvf[:, sl] + krope
        k_ref[:, sl] = ((xk * ck + k_pi) * rstd(xk)).astype(BF16)


def _inproj(x, anw, win, qaw, wqb, kvaw, wkvb, qnw, knw, cosf, sinf, tm):
    B, L, D = x.shape
    kw = N_HEADS * HEAD_PAD
    tok = lambda w: pl.BlockSpec((None, tm, w), lambda b, t: (b, t, 0))
    pos = pl.BlockSpec((tm, HEAD_PAD), lambda b, t: (t, 0))
    return pl.pallas_call(
        _inproj_body,
        grid=(B, L // tm),
        in_specs=[tok(D), _const_spec(anw.shape), _const_spec(win.shape), _const_spec(qaw.shape),
                  _const_spec(wqb.shape), _const_spec(kvaw.shape), _const_spec(wkvb.shape),
                  _const_spec(qnw.shape), _const_spec(knw.shape), pos, pos],
        out_specs=[tok(HY3), tok(kw), tok(kw), tok(kw)],
        out_shape=[jax.ShapeDtypeStruct((B, L, HY3), F32)] + [jax.ShapeDtypeStruct((B, L, kw), BF16)] * 3,
        compiler_params=_params("parallel", "parallel"),
        name="inproj",
    )(x, anw, win, qaw, wqb, kvaw, wkvb, qnw, knw, cosf, sinf)


def _gate_body(prev_ref, cur_ref, next_ref, cw_ref, cb_ref, x0_ref, z_ref):
    t = pl.program_id(1)
    nt = pl.num_programs(1)
    tl = cur_ref.shape[0]
    row = lax.broadcasted_iota(jnp.int32, (tl, HY_WIDTH), 0)
    has_prev = (t > 0).astype(F32)
    has_next = (t < nt - 1).astype(F32)

    def stream(s):
        sl = slice(s * HY_WIDTH, (s + 1) * HY_WIDTH)
        cur = cur_ref[:, sl]
        prow = prev_ref[SUBLANES - 1:SUBLANES, sl] * has_prev
        nrow = next_ref[0:1, sl] * has_next
        dn = jnp.where(row == 0, prow, pltpu.roll(cur, 1, 0))
        up = jnp.where(row == tl - 1, nrow, pltpu.roll(cur, tl - 1, 0))
        return cb_ref[:, sl] + dn * cw_ref[0:1, sl] + cur * cw_ref[1:2, sl] + up * cw_ref[2:3, sl]

    x0_ref[...] = stream(0)
    z_ref[...] = stream(2) * stream(1)


def _gate(hy, conv_w, conv_b, tl):
    B, L, _ = hy.shape
    nb = tl // SUBLANES
    last = L // SUBLANES - 1
    out = pl.BlockSpec((None, tl, HY_WIDTH), lambda b, t: (b, t, 0))
    return pl.pallas_call(
        _gate_body,
        grid=(B, L // tl),
        in_specs=[pl.BlockSpec((None, SUBLANES, HY3), lambda b, t: (b, jnp.maximum(t * nb - 1, 0), 0)),
                  pl.BlockSpec((None, tl, HY3), lambda b, t: (b, t, 0)),
                  pl.BlockSpec((None, SUBLANES, HY3), lambda b, t: (b, jnp.minimum((t + 1) * nb, last), 0)),
                  _const_spec(conv_w.shape), _const_spec(conv_b.shape)],
        out_specs=[out, out],
        out_shape=[jax.ShapeDtypeStruct((B, L, HY_WIDTH), F32)] * 2,
        compiler_params=_params("parallel", "parallel"),
        name="gate",
    )(hy, hy, hy, conv_w, conv_b)


def _filt_body(z_ref, w1_ref, b1_ref, fr_ref, w2_ref, b2_ref, w3_ref, dl_ref, o_ref):
    z = z_ref[...]
    fr = fr_ref[...]
    dot = functools.partial(jnp.dot, precision=HIGHEST, preferred_element_type=F32)
    h = jnp.sin(fr * (dot(z, w1_ref[...]) + b1_ref[...]))
    h = jnp.sin(fr * (dot(h, w2_ref[...]) + b2_ref[...]))
    h = dot(h, w3_ref[...])
    o_ref[...] = h * jnp.exp(-z[:, 0:1] * dl_ref[...])


def _filt(feat, w1, b1, fr, w2, b2, w3, deltas, tf):
    n2l = feat.shape[0]
    half = n2l // tf // 2
    return pl.pallas_call(
        _filt_body,
        grid=(n2l // tf,),
        in_specs=[pl.BlockSpec((tf, LANES), lambda t: (t, 0)), _const_spec(w1.shape), _const_spec(b1.shape),
                  _const_spec(fr.shape), _const_spec(w2.shape), _const_spec(b2.shape),
                  pl.BlockSpec((FILT_HIDDEN, HY_WIDTH), lambda t: (0, t // half)),
                  _const_spec(deltas.shape)],
        out_specs=pl.BlockSpec((tf, HY_WIDTH), lambda t: (t, 0)),
        out_shape=jax.ShapeDtypeStruct((n2l, HY_WIDTH), F32),
        compiler_params=_params("parallel"),
        name="filt",
    )(feat, w1, b1, fr, w2, b2, w3, deltas)


def _hi_lo(x):
    hi = x.astype(BF16)
    return hi, (x - hi.astype(F32)).astype(BF16)


def _dot3(m_ref, x):
    xh, xl = _hi_lo(x)
    mh = m_ref[0]
    dot = functools.partial(jnp.dot, preferred_element_type=F32)
    return dot(mh, xh) + dot(mh, xl) + dot(m_ref[1], xh)


def _dft1_body(c_ref, x_ref, o_ref):
    k, _, cw = x_ref.shape
    m = o_ref.shape[0]
    x2 = x_ref.reshape(k * SUBLANES, cw)
    o2 = o_ref.reshape(m * SUBLANES, cw)
    for s in range(SUBLANES):
        o2[pl.ds(s, m, stride=SUBLANES), :] = _dot3(c_ref, x2[pl.ds(s, k, stride=SUBLANES), :])


def _dft1(cmat, xv, cw):
    P, K, n2, C = xv.shape
    M = cmat.shape[1]
    return pl.pallas_call(
        _dft1_body,
        grid=(P, n2 // SUBLANES, C // cw),
        in_specs=[_const_spec(cmat.shape), pl.BlockSpec((None, K, SUBLANES, cw), lambda p, i, c: (p, 0, i, c))],
        out_specs=pl.BlockSpec((None, M, SUBLANES, cw), lambda p, i, c: (p, 0, i, c)),
        out_shape=jax.ShapeDtypeStruct((P, M, n2, C), F32),
        compiler_params=_params("parallel", "parallel", "parallel"),
        name="dft1",
    )(cmat, xv)


def _mid_kernel_body(m_ref, a_ref, o_ref):
    kb, n2, cw = a_ref.shape[1:]
    for k in range(kb):
        x = _dot3(m_ref.at[k], a_ref[:, k].reshape(2 * n2, cw))
        o_ref[:, k] = x.reshape(2, n2, cw)


def _mid_body(m_ref, mt_ref, kf_ref, a_ref, o_ref):
    kb, n2, cw = a_ref.shape[1:]
    for k in range(kb):
        x = _dot3(m_ref.at[k], a_ref[:, k].reshape(2 * n2, cw))
        xr, xi = x[:n2], x[n2:]
        kr, ki = kf_ref[0, k], kf_ref[1, k]
        y = jnp.concatenate([xr * kr - xi * ki, xr * ki + xi * kr], axis=0)
        o_ref[:, k] = _dot3(mt_ref.at[k], y).reshape(2, n2, cw)


def _mid(mblk, mblk_t, kf, a, kb, cw):
    P, _, n1, n2, C = a.shape
    dat = pl.BlockSpec((None, 2, kb, n2, cw), lambda k, c, p: (p, 0, k, 0, c))
    mat = pl.BlockSpec((kb, 2, 2 * n2, 2 * n2), lambda k, c, p: (k, 0, 0, 0))
    if kf is None:
        body, ins, specs = _mid_kernel_body, (mblk, a), [mat, dat]
    else:
        kspec = pl.BlockSpec((None, 2, kb, n2, cw), lambda k, c, p: (0, 0, k, 0, c))
        body, ins, specs = _mid_body, (mblk, mblk_t, kf, a), [mat, mat, kspec, dat]
    return pl.pallas_call(
        body,
        grid=(n1 // kb, C // cw, P),
        in_specs=specs,
        out_specs=dat,
        out_shape=jax.ShapeDtypeStruct(a.shape, F32),
        compiler_params=_params("parallel", "parallel", "parallel"),
        name="mid",
    )(*ins)


def _dft4_body(c_ref, b_ref, x0_ref, z_ref, bias_ref, o_ref):
    k, _, cw = b_ref.shape
    m = o_ref.shape[0]
    flat = lambda r: r.reshape(r.shape[0] * SUBLANES, cw)
    b2, x02, z2, o2 = flat(b_ref), flat(x0_ref), flat(z_ref), flat(o_ref)
    bias = bias_ref[...]
    for s in range(SUBLANES):
        rows = pl.ds(s, m, stride=SUBLANES)
        y = _dot3(c_ref, b2[pl.ds(s, k, stride=SUBLANES), :])
        o2[rows, :] = x02[rows, :] * (y + z2[rows, :] * bias)


def _dft4(cmat, bv, x0v, zv, bias, cw):
    P, K, n2, C = bv.shape
    M = cmat.shape[1]
    blk = lambda r: pl.BlockSpec((None, r, SUBLANES, cw), lambda p, i, c: (p, 0, i, c))
    return pl.pallas_call(
        _dft4_body,
        grid=(P, n2 // SUBLANES, C // cw),
        in_specs=[_const_spec(cmat.shape), blk(K), blk(M), blk(M), pl.BlockSpec((1, cw), lambda p, i, c: (0, c))],
        out_specs=blk(M),
        out_shape=jax.ShapeDtypeStruct((P, M, n2, C), F32),
        compiler_params=_params("parallel", "parallel", "parallel"),
        name="dft4",
    )(cmat, bv, x0v, zv, bias)


def _dft_tables(n1, n2):
    n = n1 * n2
    h = n1 // 2
    i1 = jnp.arange(n1, dtype=jnp.int32)
    th = (2.0 * math.pi / n1) * ((i1[:, None] * i1[None, :]) % n1).astype(F32)
    c1, s1 = jnp.cos(th), jnp.sin(th)
    cz = jnp.block([[c1[:, :h], s1[:, :h]], [-s1[:, :h], c1[:, :h]]])
    ck = jnp.concatenate([c1, -s1], axis=0)
    c4 = jnp.block([[c1[:h], -s1[:h]], [s1[:h], c1[:h]]]) * (1.0 / n)
    k1 = jnp.arange(n1, dtype=jnp.int32)[:, None, None]
    k2 = jnp.arange(n2, dtype=jnp.int32)[None, :, None]
    i2 = jnp.arange(n2, dtype=jnp.int32)[None, None, :]
    ph = (2.0 * math.pi / n) * ((i2 * (k1 + n1 * k2)) % n).astype(F32)
    mr, mi = jnp.cos(ph), -jnp.sin(ph)
    mblk = jnp.concatenate([jnp.concatenate([mr, -mi], axis=2), jnp.concatenate([mi, mr], axis=2)], axis=1)
    parts = lambda m, axis: jnp.stack(_hi_lo(m), axis=axis)
    return parts(cz, 0), parts(ck, 0), parts(c4, 0), parts(mblk, 1), parts(jnp.swapaxes(mblk, 1, 2), 1)


def _split(L):
    n = 2 * L
    n1 = 1 << ((n.bit_length() - 1) // 2)
    return n1, n // n1


def _hyena_conv(x0, z, kern, filt_bias, kb, cw):
    B, L, C = z.shape
    n1, n2 = _split(L)
    P = B // 2
    cz, ck, c4, mblk, mblk_t = _dft_tables(n1, n2)
    ka = _dft1(ck, kern.reshape(1, n1, n2, C), LANES)
    kf = _mid(mblk, None, None, ka.reshape(1, 2, n1, n2, C), kb, cw)
    za = _dft1(cz, z.reshape(P, n1, n2, C), LANES)
    zb = _mid(mblk, mblk_t, kf, za.reshape(P, 2, n1, n2, C), kb, cw)
    y = _dft4(c4, zb.reshape(P, 2 * n1, n2, C), x0.reshape(P, n1, n2, C), z.reshape(P, n1, n2, C),
              filt_bias.reshape(1, C), LANES)
    return y.reshape(B, L, C)


def _attn_body(q_ref, k_ref, v_ref, o_ref, s_scr, p_scr, m_scr, acc_scr, *, tk, rb):
    tq = q_ref.shape[0]
    nk = k_ref.shape[0] // tk
    nc = tk // LANES
    m_scr[...] = jnp.full(m_scr.shape, -jnp.inf, F32)
    acc_scr[...] = jnp.zeros(acc_scr.shape, F32)

    def step(j, carry):
        off = pl.multiple_of(j * tk, tk)
        for h in range(2):
            hs = slice(h * HEAD_PAD, (h + 1) * HEAD_PAD)
            s_scr[h] = lax.dot_general(q_ref[:, hs], k_ref[pl.ds(off, tk), hs], (((1,), (1,)), ((), ())),
                                       preferred_element_type=F32)
        for h in range(2):
            hs = slice(h * HEAD_PAD, (h + 1) * HEAD_PAD)
            for r in range(tq // rb):
                rows = slice(r * rb, (r + 1) * rb)
                mx = s_scr[h, rows, 0:LANES]
                for c in range(1, nc):
                    mx = jnp.maximum(mx, s_scr[h, rows, c * LANES:(c + 1) * LANES])
                m_old = m_scr[h, rows, :]
                m_new = jnp.maximum(m_old, jnp.max(mx, axis=-1, keepdims=True))
                m_scr[h, rows, :] = m_new
                acc_scr[h, rows, :] = acc_scr[h, rows, :] * jnp.exp2(m_old - m_new)
                for c in range(nc):
                    cs = slice(c * LANES, (c + 1) * LANES)
                    p_scr[h, rows, cs] = jnp.exp2(s_scr[h, rows, cs] - m_new).astype(BF16)
            acc_scr[h] += jnp.dot(p_scr[h], v_ref[pl.ds(off, tk), hs], preferred_element_type=F32)
        return carry

    lax.fori_loop(0, nk, step, 0)
    a0, a1 = acc_scr[0], acc_scr[1]
    lane = lax.broadcasted_iota(jnp.int32, a0.shape, 1)
    o_ref[...] = jnp.where(lane < V_DIM, a0 / a0[:, V_DIM:V_DIM + 1], a1 / a1[:, 0:1])


def _attn(q, k, v, tq, tk, rb):
    B, L, _ = q.shape
    pair = 2 * HEAD_PAD
    return pl.pallas_call(
        functools.partial(_attn_body, tk=tk, rb=rb),
        grid=(B, N_HEADS // 2, L // tq),
        in_specs=[pl.BlockSpec((None, tq, pair), lambda b, h, i: (b, i, h)),
                  pl.BlockSpec((None, L, pair), lambda b, h, i: (b, 0, h)),
                  pl.BlockSpec((None, L, pair), lambda b, h, i: (b, 0, h))],
        out_specs=pl.BlockSpec((None, tq, 2 * V_DIM), lambda b, h, i: (b, i, h)),
        out_shape=jax.ShapeDtypeStruct((B, L, N_HEADS * V_DIM), F32),
        scratch_shapes=[pltpu.VMEM((2, tq, tk), F32), pltpu.VMEM((2, tq, tk), BF16),
                        pltpu.VMEM((2, tq, HEAD_PAD), F32), pltpu.VMEM((2, tq, HEAD_PAD), F32)],
        compiler_params=_params("parallel", "parallel", "parallel"),
        name="attn",
    )(q, k, v)


def _tail_body(x_ref, yh_ref, om_ref, p_ref, hnw_ref, mnw_ref, wo_ref, mlw_ref, w1_ref, w2_ref, wg_ref,
               wp_ref, pnw_ref, o_ref, *, fc):
    dot = functools.partial(jnp.dot, preferred_element_type=F32)
    a = _rms(yh_ref[...], hnw_ref[...]).astype(BF16)
    b = _rms(om_ref[...], mnw_ref[...]).astype(BF16)
    x = x_ref[...] + dot(a, wo_ref[:HY_WIDTH, :]) + dot(b, wo_ref[HY_WIDTH:, :])
    xn = _rms(x, mlw_ref[...]).astype(BF16)
    mlp = jnp.zeros_like(x)
    for c in range(w1_ref.shape[1] // fc):
        h = jnp.maximum(dot(xn, w1_ref[:, c * fc:(c + 1) * fc]), 0.0)
        mlp = mlp + dot((h * h).astype(BF16), w2_ref[c * fc:(c + 1) * fc, :])
    x = x + mlp
    gate = jax.nn.sigmoid(dot(x.astype(BF16), wg_ref[...]))
    pp = _rms(dot(p_ref[...].astype(BF16), wp_ref[...]), pnw_ref[...])
    o_ref[...] = x + gate * pp


def _tail(x, yh, om, p, hnw, mnw, wo, mlw, w1, w2, wg, wp, pnw, tm, fc):
    B, L, D = x.shape
    tok = lambda w: pl.BlockSpec((None, tm, w), lambda b, t: (b, t, 0))
    res = lambda a: pl.BlockSpec(a.shape, lambda b, t: (0,) * a.ndim, pipeline_mode=pl.Buffered(1))
    return pl.pallas_call(
        functools.partial(_tail_body, fc=fc),
        grid=(B, L // tm),
        in_specs=[tok(D), tok(yh.shape[-1]), tok(om.shape[-1]), tok(p.shape[-1]),
                  res(hnw), res(mnw), res(wo), res(mlw), res(w1), res(w2), res(wg), res(wp), res(pnw)],
        out_specs=tok(D),
        out_shape=jax.ShapeDtypeStruct((B, L, D), F32),
        compiler_params=_params("parallel", "parallel"),
        name="tail",
    )(x, yh, om, p, hnw, mnw, wo, mlw, w1, w2, wg, wp, pnw)


def _rope_tables(L):
    inv = ROPE_THETA ** (-(2.0 * jnp.arange(HALF_ROPE, dtype=F32)) / QK_ROPE)
    ang = jnp.arange(L, dtype=F32)[:, None] * inv[None, :]
    cos, sin = jnp.cos(ang), jnp.sin(ang)
    one = jnp.ones((L, QK_NOPE), F32)
    pad = jnp.zeros((L, HEAD_PAD - QK_DIM), F32)
    cosf = jnp.concatenate([one, cos, cos, pad], axis=1)
    sinf = jnp.concatenate([0.0 * one, -sin, sin, pad], axis=1)
    return cosf, sinf


def _filter_features(L):
    t = jnp.linspace(0.0, 1.0, L, dtype=F32)[:, None]
    w = (2.0 * math.pi) * jnp.arange(L, dtype=F32)[:, None] / L
    f = jnp.linspace(1e-4, FILT_BANDS - 1, FILT_BANDS, dtype=F32)[None, :]
    z = jnp.concatenate([t, jnp.cos(f * w), -jnp.sin(f * w)], axis=-1)
    z = jnp.pad(z, ((0, 0), (0, LANES - z.shape[1])))
    return jnp.concatenate([z, z[::-1]], axis=0)


def _decay_rates():
    min_decay = math.log(DECAY_TARGET) / FAST_DECAY_PCT
    max_decay = math.log(DECAY_TARGET) / SLOW_DECAY_PCT
    return jnp.abs(jnp.linspace(min_decay, max_decay, HY_WIDTH, dtype=F32))[None, :]


def _pad_heads(w, width, lead=0):
    w = w.reshape(w.shape[:-1] + (N_HEADS, width))
    pad = [(0, 0)] * (w.ndim - 1) + [(lead, HEAD_PAD - width - lead)]
    return jnp.pad(w, pad).reshape(w.shape[:-2] + (N_HEADS * HEAD_PAD,))


def _rope_partner(w):
    a, b = w[..., QK_NOPE:QK_NOPE + HALF_ROPE], w[..., QK_NOPE + HALF_ROPE:]
    zeros = lambda n: jnp.zeros(w.shape[:-1] + (n,), w.dtype)
    return jnp.concatenate([zeros(QK_NOPE), b, a, zeros(HEAD_PAD - QK_DIM)], axis=-1)


def _layer_weights(attn_norm_w, w_in, q_a_norm_w, w_q_b, kv_a_norm_w, w_kv_b, q_norm_w, k_norm_w):
    d = w_in.shape[0]
    o3 = HY3 + Q_LORA + KV_LORA
    rope_cols = jnp.pad(w_in[:, o3:], ((0, 0), (QK_NOPE, 0)))
    win = jnp.concatenate([w_in[:, :o3], jnp.pad(rope_cols, ((0, 0), (0, HEAD_PAD - QK_DIM))),
                           _rope_partner(rope_cols)], axis=1).astype(BF16)
    wq = w_q_b.reshape(Q_LORA, N_HEADS, QK_DIM)
    wqb = jnp.concatenate([_pad_heads(w_q_b, QK_DIM), _rope_partner(wq).reshape(Q_LORA, N_HEADS * HEAD_PAD)],
                          axis=1).astype(BF16)
    kv = w_kv_b.reshape(KV_LORA, N_HEADS, QK_NOPE + V_DIM)
    wk = _pad_heads(kv[:, :, :QK_NOPE].reshape(KV_LORA, N_HEADS * QK_NOPE), QK_NOPE)
    v4 = kv[:, :, QK_NOPE:].reshape(KV_LORA, N_HEADS // 2, 2, V_DIM)
    zero = jnp.zeros_like(v4[:, :, 0])
    wv = jnp.stack([v4[:, :, 0], zero, zero, v4[:, :, 1]], axis=2).reshape(KV_LORA, N_HEADS * HEAD_PAD)
    wkvb = jnp.concatenate([wk, wv], axis=1).astype(BF16)
    norm2 = lambda w: jnp.stack([jnp.pad(w, (0, HEAD_PAD - QK_DIM)), _rope_partner(w)])
    return (attn_norm_w[None, :], win, q_a_norm_w[None, :], wqb, kv_a_norm_w[None, :], wkvb,
            norm2(q_norm_w), norm2(k_norm_w))


def _group(x, p, lw, conv_w, conv_b, filt, filt_bias, tailw):
    B, L, D = x.shape
    cosf, sinf = _rope_tables(L)
    hy, q, k, v = _inproj(x, *lw, cosf, sinf, tm=512)
    x0, z = _gate(hy, conv_w, conv_b, tl=512)
    kern = _filt(_filter_features(L), *filt, _decay_rates(), tf=512)
    yh = _hyena_conv(x0, z, kern, filt_bias, kb=4, cw=HY_WIDTH)
    om = _attn(q, k, v, tq=512, tk=min(1024, L), rb=128)
    return _tail(x, yh, om, p, *tailw, tm=512, fc=1024)


def kernel(x_prompt, x_sample, p_prompt, p_sample, attn_norm_w, w_in, conv_w, conv_b, filt_w1, filt_b1, filt_freq, filt_w2, filt_b2, filt_w3, filt_bias, q_a_norm_w, w_q_b, kv_a_norm_w, w_kv_b, q_norm_w, k_norm_w, hy_out_norm_w, mla_out_norm_w, w_out, mlp_norm_w, w_mlp1, w_mlp2, w_ple_gate, w_ple_proj, ple_norm_w):
    depth = w_in.shape[0]
    for i in range(depth):
        lw = _layer_weights(attn_norm_w[i], w_in[i], q_a_norm_w[i], w_q_b[i], kv_a_norm_w[i], w_kv_b[i],
                            q_norm_w[i], k_norm_w[i])
        w1 = jnp.pad(filt_w1[i], ((0, LANES - filt_w1.shape[1]), (0, 0)))
        filt = (w1, filt_b1[i][None, :], filt_freq[i][None, :], filt_w2[i], filt_b2[i][None, :], filt_w3[i])
        tailw = (hy_out_norm_w[i][None, :], mla_out_norm_w[i][None, :], w_out[i].astype(BF16),
                 mlp_norm_w[i][None, :], w_mlp1[i].astype(BF16), w_mlp2[i].astype(BF16),
                 w_ple_gate[i].astype(BF16), w_ple_proj[i].astype(BF16), ple_norm_w[i][None, :])
        x_prompt = _group(x_prompt, p_prompt[i], lw, conv_w[i], conv_b[i][None, :], filt, filt_bias[i], tailw)
        x_sample = _group(x_sample, p_sample[i], lw, conv_w[i], conv_b[i][None, :], filt, filt_bias[i], tailw)
    return (x_prompt, x_sample)
```

```python
import functools
import math

import jax
import jax.numpy as jnp
from jax import lax
from jax.experimental import pallas as pl
from jax.experimental.pallas import tpu as pltpu

F32 = jnp.float32
BF16 = jnp.bfloat16
HIGHEST = lax.Precision.HIGHEST

LANES = 128
SUBLANES = 8
VMEM_LIMIT = 56 * 1024 * 1024

HY_WIDTH = 512
HY_STREAMS = 3
HY3 = HY_STREAMS * HY_WIDTH
FILT_BANDS = 16
FILT_HIDDEN = 64
DECAY_TARGET = 1e-2
FAST_DECAY_PCT = 0.3
SLOW_DECAY_PCT = 1.5
N_HEADS = 8
QK_NOPE = 64
QK_ROPE = 32
QK_DIM = QK_NOPE + QK_ROPE
V_DIM = 64
Q_LORA = 256
KV_LORA = 128
ROPE_THETA = 10000.0
EPS = 1e-6
HEAD_PAD = LANES
HALF_ROPE = QK_ROPE // 2


def _rms(x, w):
    ms = jnp.mean(x * x, axis=-1, keepdims=True)
    return x * lax.rsqrt(ms + EPS) * w


def _params(*sem):
    return pltpu.CompilerParams(dimension_semantics=sem, vmem_limit_bytes=VMEM_LIMIT)


def _const_spec(shape):
    nd = len(shape)
    return pl.BlockSpec(shape, lambda *_: (0,) * nd)


def _inproj_body(xp_ref, x_ref, xn_ref, anw_ref, win_ref, cw_ref, cb_ref, qaw_ref, wqb_ref, kvaw_ref, wkvb_ref,
                 qnw_ref, knw_ref, cos_ref, sin_ref, x0_ref, z_ref, q_ref, k_ref, v_ref):
    t = pl.program_id(1)
    tm = x_ref.shape[0]
    has_prev = (t > 0).astype(F32)
    has_next = (t < pl.num_programs(1) - 1).astype(F32)
    xa = jnp.concatenate([xp_ref[...] * has_prev, x_ref[...], xn_ref[...] * has_next], axis=0)
    proj = jnp.dot(_rms(xa, anw_ref[...]).astype(BF16), win_ref[...], preferred_element_type=F32)
    rows = xa.shape[0]
    body = slice(SUBLANES, SUBLANES + tm)

    def stream(s):
        sl = slice(s * HY_WIDTH, (s + 1) * HY_WIDTH)
        h = proj[:, sl]
        dn, up = pltpu.roll(h, 1, 0)[body], pltpu.roll(h, rows - 1, 0)[body]
        return cb_ref[:, sl] + dn * cw_ref[0:1, sl] + h[body] * cw_ref[1:2, sl] + up * cw_ref[2:3, sl]

    x0_ref[...] = stream(0)
    z_ref[...] = stream(2) * stream(1)

    o2 = HY3 + Q_LORA
    o3 = o2 + KV_LORA
    o4 = o3 + HEAD_PAD
    q_lat, kv_lat, krope, krope_pi = proj[body, HY3:o2], proj[body, o2:o3], proj[body, o3:o4], proj[body, o4:]
    qf = jnp.dot(_rms(q_lat, qaw_ref[...]).astype(BF16), wqb_ref[...], preferred_element_type=F32)
    kvf = jnp.dot(_rms(kv_lat, kvaw_ref[...]).astype(BF16), wkvb_ref[...], preferred_element_type=F32)
    kw = N_HEADS * HEAD_PAD
    lane = lax.broadcasted_iota(jnp.int32, (1, kw), 1) % (2 * HEAD_PAD)
    ones_col = (lane == V_DIM) | (lane == HEAD_PAD)
    v_ref[...] = jnp.where(ones_col, 1.0, kvf[:, kw:]).astype(BF16)

    cosf, sinf = cos_ref[...], sin_ref[...]
    qscale = QK_DIM ** -0.5 * math.log2(math.e)
    cq, sq = cosf * (qnw_ref[0:1] * qscale), sinf * (qnw_ref[1:2] * qscale)
    ck, sk = cosf * knw_ref[0:1], sinf * knw_ref[1:2]
    k_pi = krope_pi * sk

    def rstd(xh):
        return lax.rsqrt(jnp.sum(xh * xh, axis=-1, keepdims=True) / QK_DIM + EPS)

    for h in range(N_HEADS):
        sl = slice(h * HEAD_PAD, (h + 1) * HEAD_PAD)
        xq = qf[:, sl]
        q_ref[:, sl] = ((xq * cq + qf[:, kw + h * HEAD_PAD:kw + (h + 1) * HEAD_PAD] * sq) * rstd(xq)).astype(BF16)
        xk = kvf[:, sl] + krope
        k_ref[:, sl] = ((xk * ck + k_pi) * rstd(xk)).astype(BF16)


def _inproj(x, anw, win, conv_w, conv_b, qaw, wqb, kvaw, wkvb, qnw, knw, cosf, sinf, tm):
    B, L, D = x.shape
    kw = N_HEADS * HEAD_PAD
    nb = tm // SUBLANES
    last = L // SUBLANES - 1
    tok = lambda w: pl.BlockSpec((None, tm, w), lambda b, t: (b, t, 0))
    pos = pl.BlockSpec((tm, HEAD_PAD), lambda b, t: (t, 0))
    consts = [anw, win, conv_w, conv_b, qaw, wqb, kvaw, wkvb, qnw, knw]
    return pl.pallas_call(
        _inproj_body,
        grid=(B, L // tm),
        in_specs=[pl.BlockSpec((None, SUBLANES, D), lambda b, t: (b, jnp.maximum(t * nb - 1, 0), 0)), tok(D),
                  pl.BlockSpec((None, SUBLANES, D), lambda b, t: (b, jnp.minimum((t + 1) * nb, last), 0))]
                 + [_const_spec(c.shape) for c in consts] + [pos, pos],
        out_specs=[tok(HY_WIDTH), tok(HY_WIDTH), tok(kw), tok(kw), tok(kw)],
        out_shape=[jax.ShapeDtypeStruct((B, L, HY_WIDTH), F32)] * 2 + [jax.ShapeDtypeStruct((B, L, kw), BF16)] * 3,
        compiler_params=_params("parallel", "parallel"),
        name="inproj",
    )(x, x, x, *consts, cosf, sinf)


def _filt_body(z_ref, w1_ref, b1_ref, fr_ref, w2_ref, b2_ref, w3_ref, dl_ref, o_ref):
    z = z_ref[...]
    fr = fr_ref[...]
    dot = functools.partial(jnp.dot, precision=HIGHEST, preferred_element_type=F32)
    h = jnp.sin(fr * (dot(z, w1_ref[...]) + b1_ref[...]))
    h = jnp.sin(fr * (dot(h, w2_ref[...]) + b2_ref[...]))
    h = dot(h, w3_ref[...])
    o_ref[...] = h * jnp.exp(-z[:, 0:1] * dl_ref[...])


def _filt(feat, w1, b1, fr, w2, b2, w3, deltas, tf):
    n2l = feat.shape[0]
    half = n2l // tf // 2
    return pl.pallas_call(
        _filt_body,
        grid=(n2l // tf,),
        in_specs=[pl.BlockSpec((tf, LANES), lambda t: (t, 0)), _const_spec(w1.shape), _const_spec(b1.shape),
                  _const_spec(fr.shape), _const_spec(w2.shape), _const_spec(b2.shape),
                  pl.BlockSpec((FILT_HIDDEN, HY_WIDTH), lambda t: (0, t // half)),
                  _const_spec(deltas.shape)],
        out_specs=pl.BlockSpec((tf, HY_WIDTH), lambda t: (t, 0)),
        out_shape=jax.ShapeDtypeStruct((n2l, HY_WIDTH), F32),
        compiler_params=_params("parallel"),
        name="filt",
    )(feat, w1, b1, fr, w2, b2, w3, deltas)


def _hi_lo(x):
    hi = x.astype(BF16)
    return hi, (x - hi.astype(F32)).astype(BF16)


def _dot3(m_ref, x):
    xh, xl = _hi_lo(x)
    mh = m_ref[0]
    dot = functools.partial(jnp.dot, preferred_element_type=F32)
    return dot(mh, xh) + dot(mh, xl) + dot(m_ref[1], xh)


def _dft1_body(c_ref, x_ref, o_ref):
    k, rb, cw = x_ref.shape
    m = o_ref.shape[0]
    x2 = x_ref.reshape(k * rb, cw)
    o2 = o_ref.reshape(m * rb, cw)
    for s in range(rb):
        o2[pl.ds(s, m, stride=rb), :] = _dot3(c_ref, x2[pl.ds(s, k, stride=rb), :])


def _dft1(cmat, xv, rb, cw):
    P, K, n2, C = xv.shape
    M = cmat.shape[1]
    return pl.pallas_call(
        _dft1_body,
        grid=(P, n2 // rb, C // cw),
        in_specs=[_const_spec(cmat.shape), pl.BlockSpec((None, K, rb, cw), lambda p, i, c: (p, 0, i, c))],
        out_specs=pl.BlockSpec((None, M, rb, cw), lambda p, i, c: (p, 0, i, c)),
        out_shape=jax.ShapeDtypeStruct((P, M, n2, C), F32),
        compiler_params=_params("parallel", "parallel", "parallel"),
        name="dft1",
    )(cmat, xv)


def _mid_kernel_body(m_ref, a_ref, o_ref):
    kb, n2, cw = a_ref.shape[1:]
    for k in range(kb):
        x = _dot3(m_ref.at[k], a_ref[:, k].reshape(2 * n2, cw))
        o_ref[:, k] = x.reshape(2, n2, cw)


def _mid_body(m_ref, mt_ref, kf_ref, a_ref, o_ref):
    kb, n2, cw = a_ref.shape[1:]
    for k in range(kb):
        x = _dot3(m_ref.at[k], a_ref[:, k].reshape(2 * n2, cw))
        xr, xi = x[:n2], x[n2:]
        kr, ki = kf_ref[0, k], kf_ref[1, k]
        y = jnp.concatenate([xr * kr - xi * ki, xr * ki + xi * kr], axis=0)
        o_ref[:, k] = _dot3(mt_ref.at[k], y).reshape(2, n2, cw)


def _mid(mblk, mblk_t, kf, a, kb, cw):
    P, _, n1, n2, C = a.shape
    dat = pl.BlockSpec((None, 2, kb, n2, cw), lambda k, c, p: (p, 0, k, 0, c))
    mat = pl.BlockSpec((kb, 2, 2 * n2, 2 * n2), lambda k, c, p: (k, 0, 0, 0))
    if kf is None:
        body, ins, specs = _mid_kernel_body, (mblk, a), [mat, dat]
    else:
        kspec = pl.BlockSpec((None, 2, kb, n2, cw), lambda k, c, p: (0, 0, k, 0, c))
        body, ins, specs = _mid_body, (mblk, mblk_t, kf, a), [mat, mat, kspec, dat]
    return pl.pallas_call(
        body,
        grid=(n1 // kb, C // cw, P),
        in_specs=specs,
        out_specs=dat,
        out_shape=jax.ShapeDtypeStruct(a.shape, F32),
        compiler_params=_params("parallel", "parallel", "parallel"),
        name="mid",
    )(*ins)


def _dft4_body(c_ref, b_ref, x0_ref, z_ref, bias_ref, o_ref):
    k, rb, cw = b_ref.shape
    m = o_ref.shape[0]
    flat = lambda r: r.reshape(r.shape[0] * rb, cw)
    b2, x02, z2, o2 = flat(b_ref), flat(x0_ref), flat(z_ref), flat(o_ref)
    bias = bias_ref[...]
    for s in range(rb):
        rows = pl.ds(s, m, stride=rb)
        y = _dot3(c_ref, b2[pl.ds(s, k, stride=rb), :])
        o2[rows, :] = x02[rows, :] * (y + z2[rows, :] * bias)


def _dft4(cmat, bv, x0v, zv, bias, rb, cw):
    P, K, n2, C = bv.shape
    M = cmat.shape[1]
    blk = lambda r: pl.BlockSpec((None, r, rb, cw), lambda p, i, c: (p, 0, i, c))
    return pl.pallas_call(
        _dft4_body,
        grid=(P, n2 // rb, C // cw),
        in_specs=[_const_spec(cmat.shape), blk(K), blk(M), blk(M), pl.BlockSpec((1, cw), lambda p, i, c: (0, c))],
        out_specs=blk(M),
        out_shape=jax.ShapeDtypeStruct((P, M, n2, C), F32),
        compiler_params=_params("parallel", "parallel", "parallel"),
        name="dft4",
    )(cmat, bv, x0v, zv, bias)


def _dft_tables(n1, n2):
    n = n1 * n2
    h = n1 // 2
    i1 = jnp.arange(n1, dtype=jnp.int32)
    th = (2.0 * math.pi / n1) * ((i1[:, None] * i1[None, :]) % n1).astype(F32)
    c1, s1 = jnp.cos(th), jnp.sin(th)
    cz = jnp.block([[c1[:, :h], s1[:, :h]], [-s1[:, :h], c1[:, :h]]])
    ck = jnp.concatenate([c1, -s1], axis=0)
    c4 = jnp.block([[c1[:h], -s1[:h]], [s1[:h], c1[:h]]]) * (1.0 / n)
    k1 = jnp.arange(n1, dtype=jnp.int32)[:, None, None]
    u = jnp.arange(n2, dtype=jnp.int32)

    def level2(k2, i2):
        ph = (2.0 * math.pi / n) * ((i2 * (k1 + n1 * k2)) % n).astype(F32)
        mr, mi = jnp.cos(ph), -jnp.sin(ph)
        return mr, mi

    mr, mi = level2(u[None, :, None], u[None, None, :])
    mblk = jnp.concatenate([jnp.concatenate([mr, -mi], axis=2), jnp.concatenate([mi, mr], axis=2)], axis=1)
    tr, ti = level2(u[None, None, :], u[None, :, None])
    mblk_t = jnp.concatenate([jnp.concatenate([tr, ti], axis=2), jnp.concatenate([-ti, tr], axis=2)], axis=1)
    parts = lambda m, axis: jnp.stack(_hi_lo(m), axis=axis)
    return parts(cz, 0), parts(ck, 0), parts(c4, 0), parts(mblk, 1), parts(mblk_t, 1)


def _split(L):
    n = 2 * L
    n1 = 1 << ((n.bit_length() - 1) // 2)
    return n1, n // n1


def _hyena_conv(x0, z, kern, filt_bias, kb, rb, cw):
    B, L, C = z.shape
    n1, n2 = _split(L)
    P = B // 2
    cz, ck, c4, mblk, mblk_t = _dft_tables(n1, n2)
    ka = _dft1(ck, kern.reshape(1, n1, n2, C), rb, LANES)
    kf = _mid(mblk, None, None, ka.reshape(1, 2, n1, n2, C), kb, cw)
    za = _dft1(cz, z.reshape(P, n1, n2, C), rb, LANES)
    zb = _mid(mblk, mblk_t, kf, za.reshape(P, 2, n1, n2, C), kb, cw)
    y = _dft4(c4, zb.reshape(P, 2 * n1, n2, C), x0.reshape(P, n1, n2, C), z.reshape(P, n1, n2, C),
              filt_bias.reshape(1, C), rb, LANES)
    return y.reshape(B, L, C)


def _attn_body(q_ref, k_ref, v_ref, o_ref, s_scr, p_scr, m_scr, acc_scr, *, tk, rb):
    tq = q_ref.shape[0]
    nk = k_ref.shape[0] // tk
    nc = tk // LANES
    m_scr[...] = jnp.full(m_scr.shape, -jnp.inf, F32)
    acc_scr[...] = jnp.zeros(acc_scr.shape, F32)

    def step(j, carry):
        off = pl.multiple_of(j * tk, tk)
        for h in range(2):
            hs = slice(h * HEAD_PAD, (h + 1) * HEAD_PAD)
            s_scr[h] = lax.dot_general(q_ref[:, hs], k_ref[pl.ds(off, tk), hs], (((1,), (1,)), ((), ())),
                                       preferred_element_type=F32)
        for h in range(2):
            hs = slice(h * HEAD_PAD, (h + 1) * HEAD_PAD)
            for r in range(tq // rb):
                rows = slice(r * rb, (r + 1) * rb)
                mx = s_scr[h, rows, 0:LANES]
                for c in range(1, nc):
                    mx = jnp.maximum(mx, s_scr[h, rows, c * LANES:(c + 1) * LANES])
                m_old = m_scr[h, rows, :]
                m_new = jnp.maximum(m_old, jnp.max(mx, axis=-1, keepdims=True))
                m_scr[h, rows, :] = m_new
                acc_scr[h, rows, :] = acc_scr[h, rows, :] * jnp.exp2(m_old - m_new)
                for c in range(nc):
                    cs = slice(c * LANES, (c + 1) * LANES)
                    p_scr[h, rows, cs] = jnp.exp2(s_scr[h, rows, cs] - m_new).astype(BF16)
            acc_scr[h] += jnp.dot(p_scr[h], v_ref[pl.ds(off, tk), hs], preferred_element_type=F32)
        return carry

    lax.fori_loop(0, nk, step, 0, unroll=2)
    a0, a1 = acc_scr[0], acc_scr[1]
    lane = lax.broadcasted_iota(jnp.int32, a0.shape, 1)
    o_ref[...] = jnp.where(lane < V_DIM, a0 / a0[:, V_DIM:V_DIM + 1], a1 / a1[:, 0:1])


def _attn(q, k, v, tq, tk, rb):
    B, L, _ = q.shape
    pair = 2 * HEAD_PAD
    return pl.pallas_call(
        functools.partial(_attn_body, tk=tk, rb=rb),
        grid=(B, N_HEADS // 2, L // tq),
        in_specs=[pl.BlockSpec((None, tq, pair), lambda b, h, i: (b, i, h)),
                  pl.BlockSpec((None, L, pair), lambda b, h, i: (b, 0, h)),
                  pl.BlockSpec((None, L, pair), lambda b, h, i: (b, 0, h))],
        out_specs=pl.BlockSpec((None, tq, 2 * V_DIM), lambda b, h, i: (b, i, h)),
        out_shape=jax.ShapeDtypeStruct((B, L, N_HEADS * V_DIM), F32),
        scratch_shapes=[pltpu.VMEM((2, tq, tk), F32), pltpu.VMEM((2, tq, tk), BF16),
                        pltpu.VMEM((2, tq, HEAD_PAD), F32), pltpu.VMEM((2, tq, HEAD_PAD), F32)],
        compiler_params=_params("parallel", "parallel", "parallel"),
        name="attn",
    )(q, k, v)


def _tail_body(x_ref, yh_ref, om_ref, p_ref, hnw_ref, mnw_ref, wo_ref, mlw_ref, w1_ref, w2_ref, wg_ref,
               wp_ref, pnw_ref, o_ref, *, fc):
    dot = functools.partial(jnp.dot, preferred_element_type=F32)
    a = _rms(yh_ref[...], hnw_ref[...]).astype(BF16)
    b = _rms(om_ref[...], mnw_ref[...]).astype(BF16)
    x = x_ref[...] + dot(a, wo_ref[:HY_WIDTH, :]) + dot(b, wo_ref[HY_WIDTH:, :])
    xn = _rms(x, mlw_ref[...]).astype(BF16)
    mlp = jnp.zeros_like(x)
    for c in range(w1_ref.shape[1] // fc):
        h = jnp.maximum(dot(xn, w1_ref[:, c * fc:(c + 1) * fc]), 0.0)
        mlp = mlp + dot((h * h).astype(BF16), w2_ref[c * fc:(c + 1) * fc, :])
    x = x + mlp
    gate = jax.nn.sigmoid(dot(x.astype(BF16), wg_ref[...]))
    pp = _rms(dot(p_ref[...].astype(BF16), wp_ref[...]), pnw_ref[...])
    o_ref[...] = x + gate * pp


def _tail(x, yh, om, p, hnw, mnw, wo, mlw, w1, w2, wg, wp, pnw, tm, fc):
    B, L, D = x.shape
    tok = lambda w: pl.BlockSpec((None, tm, w), lambda b, t: (b, t, 0))
    res = lambda a: pl.BlockSpec(a.shape, lambda b, t: (0,) * a.ndim, pipeline_mode=pl.Buffered(1))
    return pl.pallas_call(
        functools.partial(_tail_body, fc=fc),
        grid=(B, L // tm),
        in_specs=[tok(D), tok(yh.shape[-1]), tok(om.shape[-1]), tok(p.shape[-1]),
                  res(hnw), res(mnw), res(wo), res(mlw), res(w1), res(w2), res(wg), res(wp), res(pnw)],
        out_specs=tok(D),
        out_shape=jax.ShapeDtypeStruct((B, L, D), F32),
        compiler_params=_params("parallel", "parallel"),
        name="tail",
    )(x, yh, om, p, hnw, mnw, wo, mlw, w1, w2, wg, wp, pnw)


def _rope_tables(L):
    inv = ROPE_THETA ** (-(2.0 * jnp.arange(HALF_ROPE, dtype=F32)) / QK_ROPE)
    ang = jnp.arange(L, dtype=F32)[:, None] * inv[None, :]
    cos, sin = jnp.cos(ang), jnp.sin(ang)
    one = jnp.ones((L, QK_NOPE), F32)
    pad = jnp.zeros((L, HEAD_PAD - QK_DIM), F32)
    cosf = jnp.concatenate([one, cos, cos, pad], axis=1)
    sinf = jnp.concatenate([0.0 * one, -sin, sin, pad], axis=1)
    return cosf, sinf


def _filter_features(L):
    t = jnp.linspace(0.0, 1.0, L, dtype=F32)[:, None]
    w = (2.0 * math.pi) * jnp.arange(L, dtype=F32)[:, None] / L
    f = jnp.linspace(1e-4, FILT_BANDS - 1, FILT_BANDS, dtype=F32)[None, :]
    z = jnp.concatenate([t, jnp.cos(f * w), -jnp.sin(f * w)], axis=-1)
    z = jnp.pad(z, ((0, 0), (0, LANES - z.shape[1])))
    return jnp.concatenate([z, z[::-1]], axis=0)


def _decay_rates():
    min_decay = math.log(DECAY_TARGET) / FAST_DECAY_PCT
    max_decay = math.log(DECAY_TARGET) / SLOW_DECAY_PCT
    return jnp.abs(jnp.linspace(min_decay, max_decay, HY_WIDTH, dtype=F32))[None, :]


def _pad_heads(w, width, lead=0):
    w = w.reshape(w.shape[:-1] + (N_HEADS, width))
    pad = [(0, 0)] * (w.ndim - 1) + [(lead, HEAD_PAD - width - lead)]
    return jnp.pad(w, pad).reshape(w.shape[:-2] + (N_HEADS * HEAD_PAD,))


def _rope_partner(w):
    a, b = w[..., QK_NOPE:QK_NOPE + HALF_ROPE], w[..., QK_NOPE + HALF_ROPE:]
    zeros = lambda n: jnp.zeros(w.shape[:-1] + (n,), w.dtype)
    return jnp.concatenate([zeros(QK_NOPE), b, a, zeros(HEAD_PAD - QK_DIM)], axis=-1)


def _layer_weights(attn_norm_w, w_in, q_a_norm_w, w_q_b, kv_a_norm_w, w_kv_b, q_norm_w, k_norm_w):
    d = w_in.shape[0]
    o3 = HY3 + Q_LORA + KV_LORA
    rope_cols = jnp.pad(w_in[:, o3:], ((0, 0), (QK_NOPE, 0)))
    win = jnp.concatenate([w_in[:, :o3], jnp.pad(rope_cols, ((0, 0), (0, HEAD_PAD - QK_DIM))),
                           _rope_partner(rope_cols)], axis=1).astype(BF16)
    wq = w_q_b.reshape(Q_LORA, N_HEADS, QK_DIM)
    wqb = jnp.concatenate([_pad_heads(w_q_b, QK_DIM), _rope_partner(wq).reshape(Q_LORA, N_HEADS * HEAD_PAD)],
                          axis=1).astype(BF16)
    kv = w_kv_b.reshape(KV_LORA, N_HEADS, QK_NOPE + V_DIM)
    wk = _pad_heads(kv[:, :, :QK_NOPE].reshape(KV_LORA, N_HEADS * QK_NOPE), QK_NOPE)
    v4 = kv[:, :, QK_NOPE:].reshape(KV_LORA, N_HEADS // 2, 2, V_DIM)
    zero = jnp.zeros_like(v4[:, :, 0])
    wv = jnp.stack([v4[:, :, 0], zero, zero, v4[:, :, 1]], axis=2).reshape(KV_LORA, N_HEADS * HEAD_PAD)
    wkvb = jnp.concatenate([wk, wv], axis=1).astype(BF16)
    norm2 = lambda w: jnp.stack([jnp.pad(w, (0, HEAD_PAD - QK_DIM)), _rope_partner(w)])
    return (attn_norm_w[None, :], win, q_a_norm_w[None, :], wqb, kv_a_norm_w[None, :], wkvb,
            norm2(q_norm_w), norm2(k_norm_w))


def _group(x, p, lw, conv_w, conv_b, filt, filt_bias, tailw):
    B, L, D = x.shape
    cosf, sinf = _rope_tables(L)
    x0, z, q, k, v = _inproj(x, lw[0], lw[1], conv_w, conv_b, *lw[2:], cosf, sinf, tm=512)
    kern = _filt(_filter_features(L), *filt, _decay_rates(), tf=512)
    yh = _hyena_conv(x0, z, kern, filt_bias, kb=4, rb=16, cw=HY_WIDTH)
    om = _attn(q, k, v, tq=min(1024, L), tk=min(1024, L), rb=128)
    return _tail(x, yh, om, p, *tailw, tm=512, fc=1024)


def kernel(x_prompt, x_sample, p_prompt, p_sample, attn_norm_w, w_in, conv_w, conv_b, filt_w1, filt_b1, filt_freq, filt_w2, filt_b2, filt_w3, filt_bias, q_a_norm_w, w_q_b, kv_a_norm_w, w_kv_b, q_norm_w, k_norm_w, hy_out_norm_w, mla_out_norm_w, w_out, mlp_norm_w, w_mlp1, w_mlp2, w_ple_gate, w_ple_proj, ple_norm_w):
    depth = w_in.shape[0]
    for i in range(depth):
        lw = _layer_weights(attn_norm_w[i], w_in[i], q_a_norm_w[i], w_q_b[i], kv_a_norm_w[i], w_kv_b[i],
                            q_norm_w[i], k_norm_w[i])
        w1 = jnp.pad(filt_w1[i], ((0, LANES - filt_w1.shape[1]), (0, 0)))
        filt = (w1, filt_b1[i][None, :], filt_freq[i][None, :], filt_w2[i], filt_b2[i][None, :], filt_w3[i])
        tailw = (hy_out_norm_w[i][None, :], mla_out_norm_w[i][None, :], w_out[i].astype(BF16),
                 mlp_norm_w[i][None, :], w_mlp1[i].astype(BF16), w_mlp2[i].astype(BF16),
                 w_ple_gate[i].astype(BF16), w_ple_proj[i].astype(BF16), ple_norm_w[i][None, :])
        x_prompt = _group(x_prompt, p_prompt[i], lw, conv_w[i], conv_b[i][None, :], filt, filt_bias[i], tailw)
        x_sample = _group(x_sample, p_sample[i], lw, conv_w[i], conv_b[i][None, :], filt, filt_bias[i], tailw)
    return (x_prompt, x_sample)
```

```python
import functools
import math

import jax
import jax.numpy as jnp
from jax import lax
from jax.experimental import pallas as pl
from jax.experimental.pallas import tpu as pltpu

F32 = jnp.float32
BF16 = jnp.bfloat16
HIGHEST = lax.Precision.HIGHEST

LANES = 128
SUBLANES = 8
VMEM_LIMIT = 56 * 1024 * 1024

HY_WIDTH = 512
HY_STREAMS = 3
HY3 = HY_STREAMS * HY_WIDTH
FILT_BANDS = 16
FILT_HIDDEN = 64
DECAY_TARGET = 1e-2
FAST_DECAY_PCT = 0.3
SLOW_DECAY_PCT = 1.5
N_HEADS = 8
QK_NOPE = 64
QK_ROPE = 32
QK_DIM = QK_NOPE + QK_ROPE
V_DIM = 64
Q_LORA = 256
KV_LORA = 128
ROPE_THETA = 10000.0
EPS = 1e-6
HEAD_PAD = LANES
HALF_ROPE = QK_ROPE // 2


def _rms(x, w):
    ms = jnp.mean(x * x, axis=-1, keepdims=True)
    return x * lax.rsqrt(ms + EPS) * w


def _params(*sem):
    return pltpu.CompilerParams(dimension_semantics=sem, vmem_limit_bytes=VMEM_LIMIT)


def _const_spec(shape):
    nd = len(shape)
    return pl.BlockSpec(shape, lambda *_: (0,) * nd)


def _inproj_body(xp_ref, x_ref, xn_ref, anw_ref, win_ref, cw_ref, cb_ref, qaw_ref, wqb_ref, kvaw_ref, wkvb_ref,
                 qnw_ref, knw_ref, cos_ref, sin_ref, x0_ref, z_ref, q_ref, k_ref, v_ref):
    t = pl.program_id(1)
    tm = x_ref.shape[0]
    has_prev = (t > 0).astype(F32)
    has_next = (t < pl.num_programs(1) - 1).astype(F32)
    xa = jnp.concatenate([xp_ref[...] * has_prev, x_ref[...], xn_ref[...] * has_next], axis=0)
    proj = jnp.dot(_rms(xa, anw_ref[...]).astype(BF16), win_ref[...], preferred_element_type=F32)
    rows = xa.shape[0]
    body = slice(SUBLANES, SUBLANES + tm)

    def stream(s):
        sl = slice(s * HY_WIDTH, (s + 1) * HY_WIDTH)
        h = proj[:, sl]
        dn, up = pltpu.roll(h, 1, 0)[body], pltpu.roll(h, rows - 1, 0)[body]
        return cb_ref[:, sl] + dn * cw_ref[0:1, sl] + h[body] * cw_ref[1:2, sl] + up * cw_ref[2:3, sl]

    x0_ref[...] = stream(0)
    z_ref[...] = stream(2) * stream(1)

    o2 = HY3 + Q_LORA
    o3 = o2 + KV_LORA
    o4 = o3 + HEAD_PAD
    q_lat, kv_lat, krope, krope_pi = proj[body, HY3:o2], proj[body, o2:o3], proj[body, o3:o4], proj[body, o4:]
    qf = jnp.dot(_rms(q_lat, qaw_ref[...]).astype(BF16), wqb_ref[...], preferred_element_type=F32)
    kvf = jnp.dot(_rms(kv_lat, kvaw_ref[...]).astype(BF16), wkvb_ref[...], preferred_element_type=F32)
    kw = N_HEADS * HEAD_PAD
    lane = lax.broadcasted_iota(jnp.int32, (1, kw), 1) % (2 * HEAD_PAD)
    ones_col = (lane == V_DIM) | (lane == HEAD_PAD)
    v_ref[...] = jnp.where(ones_col, 1.0, kvf[:, kw:]).astype(BF16)

    cosf, sinf = cos_ref[...], sin_ref[...]
    qscale = QK_DIM ** -0.5 * math.log2(math.e)
    cq, sq = cosf * (qnw_ref[0:1] * qscale), sinf * (qnw_ref[1:2] * qscale)
    ck, sk = cosf * knw_ref[0:1], sinf * knw_ref[1:2]
    k_pi = krope_pi * sk

    def rstd(xh):
        return lax.rsqrt(jnp.sum(xh * xh, axis=-1, keepdims=True) / QK_DIM + EPS)

    for h in range(N_HEADS):
        sl = slice(h * HEAD_PAD, (h + 1) * HEAD_PAD)
        xq = qf[:, sl]
        q_ref[:, sl] = ((xq * cq + qf[:, kw + h * HEAD_PAD:kw + (h + 1) * HEAD_PAD] * sq) * rstd(xq)).astype(BF16)
        xk = kvf[:, sl] + krope
        k_ref[:, sl] = ((xk * ck + k_pi) * rstd(xk)).astype(BF16)


def _inproj(x, anw, win, conv_w, conv_b, qaw, wqb, kvaw, wkvb, qnw, knw, cosf, sinf, tm):
    B, L, D = x.shape
    kw = N_HEADS * HEAD_PAD
    nb = tm // SUBLANES
    last = L // SUBLANES - 1
    tok = lambda w: pl.BlockSpec((None, tm, w), lambda b, t: (b, t, 0))
    pos = pl.BlockSpec((tm, HEAD_PAD), lambda b, t: (t, 0))
    consts = [anw, win, conv_w, conv_b, qaw, wqb, kvaw, wkvb, qnw, knw]
    return pl.pallas_call(
        _inproj_body,
        grid=(B, L // tm),
        in_specs=[pl.BlockSpec((None, SUBLANES, D), lambda b, t: (b, jnp.maximum(t * nb - 1, 0), 0)), tok(D),
                  pl.BlockSpec((None, SUBLANES, D), lambda b, t: (b, jnp.minimum((t + 1) * nb, last), 0))]
                 + [_const_spec(c.shape) for c in consts] + [pos, pos],
        out_specs=[tok(HY_WIDTH), tok(HY_WIDTH), tok(kw), tok(kw), tok(kw)],
        out_shape=[jax.ShapeDtypeStruct((B, L, HY_WIDTH), F32)] * 2 + [jax.ShapeDtypeStruct((B, L, kw), BF16)] * 3,
        compiler_params=_params("parallel", "parallel"),
        name="inproj",
    )(x, x, x, *consts, cosf, sinf)


def _filt_body(z_ref, w1_ref, b1_ref, fr_ref, w2_ref, b2_ref, w3_ref, dl_ref, o_ref):
    z = z_ref[...]
    fr = fr_ref[...]
    dot = functools.partial(jnp.dot, precision=HIGHEST, preferred_element_type=F32)
    h = jnp.sin(fr * (dot(z, w1_ref[...]) + b1_ref[...]))
    h = jnp.sin(fr * (dot(h, w2_ref[...]) + b2_ref[...]))
    h = dot(h, w3_ref[...])
    o_ref[...] = h * jnp.exp(-z[:, 0:1] * dl_ref[...])


def _filt(feat, w1, b1, fr, w2, b2, w3, deltas, tf):
    n2l = feat.shape[0]
    half = n2l // tf // 2
    return pl.pallas_call(
        _filt_body,
        grid=(n2l // tf,),
        in_specs=[pl.BlockSpec((tf, LANES), lambda t: (t, 0)), _const_spec(w1.shape), _const_spec(b1.shape),
                  _const_spec(fr.shape), _const_spec(w2.shape), _const_spec(b2.shape),
                  pl.BlockSpec((FILT_HIDDEN, HY_WIDTH), lambda t: (0, t // half)),
                  _const_spec(deltas.shape)],
        out_specs=pl.BlockSpec((tf, HY_WIDTH), lambda t: (t, 0)),
        out_shape=jax.ShapeDtypeStruct((n2l, HY_WIDTH), F32),
        compiler_params=_params("parallel"),
        name="filt",
    )(feat, w1, b1, fr, w2, b2, w3, deltas)


def _hi_lo(x):
    hi = x.astype(BF16)
    return hi, (x - hi.astype(F32)).astype(BF16)


def _dot3(m_ref, x):
    xh, xl = _hi_lo(x)
    mh = m_ref[0]
    dot = functools.partial(jnp.dot, preferred_element_type=F32)
    return dot(mh, xh) + dot(mh, xl) + dot(m_ref[1], xh)


def _dft1_body(c_ref, x_ref, o_ref):
    k, rb, cw = x_ref.shape
    m = o_ref.shape[0]
    x2 = x_ref.reshape(k * rb, cw)
    o2 = o_ref.reshape(m * rb, cw)
    for s in range(rb):
        o2[pl.ds(s, m, stride=rb), :] = _dot3(c_ref, x2[pl.ds(s, k, stride=rb), :])


def _dft1(cmat, xv, rb, cw):
    P, K, n2, C = xv.shape
    M = cmat.shape[1]
    return pl.pallas_call(
        _dft1_body,
        grid=(P, n2 // rb, C // cw),
        in_specs=[_const_spec(cmat.shape), pl.BlockSpec((None, K, rb, cw), lambda p, i, c: (p, 0, i, c))],
        out_specs=pl.BlockSpec((None, M, rb, cw), lambda p, i, c: (p, 0, i, c)),
        out_shape=jax.ShapeDtypeStruct((P, M, n2, C), F32),
        compiler_params=_params("parallel", "parallel", "parallel"),
        name="dft1",
    )(cmat, xv)


def _cmul(ar, ai, br, bi):
    return ar * br - ai * bi, ar * bi + ai * br


def _fwd2(f_ref, tw_ref, a_ref, k):
    n2, cw = a_ref.shape[2:]
    reps = cw // LANES
    twr = jnp.concatenate([tw_ref[0, k]] * reps, axis=1)
    twi = jnp.concatenate([tw_ref[1, k]] * reps, axis=1)
    ar, ai = _cmul(a_ref[0, k], a_ref[1, k], twr, twi)
    x = _dot3(f_ref, jnp.concatenate([ar, ai], axis=0))
    return x[:n2], x[n2:], twr, twi


def _mid_kernel_body(f_ref, tw_ref, a_ref, o_ref):
    for k in range(a_ref.shape[1]):
        o_ref[0, k], o_ref[1, k], _, _ = _fwd2(f_ref, tw_ref, a_ref, k)


def _mid_body(f_ref, ft_ref, tw_ref, kf_ref, a_ref, o_ref):
    n2 = a_ref.shape[2]
    for k in range(a_ref.shape[1]):
        xr, xi, twr, twi = _fwd2(f_ref, tw_ref, a_ref, k)
        yr, yi = _cmul(xr, xi, kf_ref[0, k], kf_ref[1, k])
        b = _dot3(ft_ref, jnp.concatenate([yr, yi], axis=0))
        o_ref[0, k], o_ref[1, k] = _cmul(b[:n2], b[n2:], twr, -twi)


def _mid(f2, f2_t, tw, kf, a, kb, cw):
    P, _, n1, n2, C = a.shape
    dat = pl.BlockSpec((None, 2, kb, n2, cw), lambda k, c, p: (p, 0, k, 0, c))
    tws = pl.BlockSpec((2, kb, n2, LANES), lambda k, c, p: (0, k, 0, 0))
    mat = _const_spec(f2.shape)
    if kf is None:
        body, ins, specs = _mid_kernel_body, (f2, tw, a), [mat, tws, dat]
    else:
        kspec = pl.BlockSpec((None, 2, kb, n2, cw), lambda k, c, p: (0, 0, k, 0, c))
        body, ins, specs = _mid_body, (f2, f2_t, tw, kf, a), [mat, mat, tws, kspec, dat]
    return pl.pallas_call(
        body,
        grid=(n1 // kb, C // cw, P),
        in_specs=specs,
        out_specs=dat,
        out_shape=jax.ShapeDtypeStruct(a.shape, F32),
        compiler_params=_params("parallel", "parallel", "parallel"),
        name="mid",
    )(*ins)


def _dft4_body(c_ref, b_ref, x0_ref, z_ref, bias_ref, o_ref):
    k, rb, cw = b_ref.shape
    m = o_ref.shape[0]
    flat = lambda r: r.reshape(r.shape[0] * rb, cw)
    b2, x02, z2, o2 = flat(b_ref), flat(x0_ref), flat(z_ref), flat(o_ref)
    bias = bias_ref[...]
    for s in range(rb):
        rows = pl.ds(s, m, stride=rb)
        y = _dot3(c_ref, b2[pl.ds(s, k, stride=rb), :])
        o2[rows, :] = x02[rows, :] * (y + z2[rows, :] * bias)


def _dft4(cmat, bv, x0v, zv, bias, rb, cw):
    P, K, n2, C = bv.shape
    M = cmat.shape[1]
    blk = lambda r: pl.BlockSpec((None, r, rb, cw), lambda p, i, c: (p, 0, i, c))
    return pl.pallas_call(
        _dft4_body,
        grid=(P, n2 // rb, C // cw),
        in_specs=[_const_spec(cmat.shape), blk(K), blk(M), blk(M), pl.BlockSpec((1, cw), lambda p, i, c: (0, c))],
        out_specs=blk(M),
        out_shape=jax.ShapeDtypeStruct((P, M, n2, C), F32),
        compiler_params=_params("parallel", "parallel", "parallel"),
        name="dft4",
    )(cmat, bv, x0v, zv, bias)


def _dft_tables(n1, n2):
    n = n1 * n2
    h = n1 // 2
    i1 = jnp.arange(n1, dtype=jnp.int32)
    th = (2.0 * math.pi / n1) * ((i1[:, None] * i1[None, :]) % n1).astype(F32)
    c1, s1 = jnp.cos(th), jnp.sin(th)
    cz = jnp.block([[c1[:, :h], s1[:, :h]], [-s1[:, :h], c1[:, :h]]])
    ck = jnp.concatenate([c1, -s1], axis=0)
    c4 = jnp.block([[c1[:h], -s1[:h]], [s1[:h], c1[:h]]]) * (1.0 / n)
    u = jnp.arange(n2, dtype=jnp.int32)
    ph2 = (2.0 * math.pi / n2) * ((u[:, None] * u[None, :]) % n2).astype(F32)
    c2, s2 = jnp.cos(ph2), jnp.sin(ph2)
    f2 = jnp.block([[c2, s2], [-s2, c2]])
    pht = (2.0 * math.pi / n) * (i1[:, None] * u[None, :]).astype(F32)
    tw = jnp.stack([jnp.cos(pht), -jnp.sin(pht)])
    tw = jnp.broadcast_to(tw[..., None], tw.shape + (LANES,))
    parts = lambda m: jnp.stack(_hi_lo(m))
    return parts(cz), parts(ck), parts(c4), parts(f2), parts(f2.T), tw


def _split(L):
    n = 2 * L
    n1 = 1 << ((n.bit_length() - 1) // 2)
    return n1, n // n1


def _hyena_conv(x0, z, kern, filt_bias, kb, rb, cw):
    B, L, C = z.shape
    n1, n2 = _split(L)
    P = B // 2
    cz, ck, c4, f2, f2_t, tw = _dft_tables(n1, n2)
    ka = _dft1(ck, kern.reshape(1, n1, n2, C), rb, LANES)
    kf = _mid(f2, None, tw, None, ka.reshape(1, 2, n1, n2, C), kb, cw)
    za = _dft1(cz, z.reshape(P, n1, n2, C), rb, LANES)
    zb = _mid(f2, f2_t, tw, kf, za.reshape(P, 2, n1, n2, C), kb, cw)
    y = _dft4(c4, zb.reshape(P, 2 * n1, n2, C), x0.reshape(P, n1, n2, C), z.reshape(P, n1, n2, C),
              filt_bias.reshape(1, C), rb, LANES)
    return y.reshape(B, L, C)


def _attn_body(q_ref, k_ref, v_ref, o_ref, s_scr, p_scr, m_scr, acc_scr, *, tk, rb):
    tq = q_ref.shape[0]
    nk = k_ref.shape[0] // tk
    nc = tk // LANES
    m_scr[...] = jnp.full(m_scr.shape, -jnp.inf, F32)
    acc_scr[...] = jnp.zeros(acc_scr.shape, F32)

    def step(j, carry):
        off = pl.multiple_of(j * tk, tk)
        for h in range(2):
            hs = slice(h * HEAD_PAD, (h + 1) * HEAD_PAD)
            s_scr[h] = lax.dot_general(q_ref[:, hs], k_ref[pl.ds(off, tk), hs], (((1,), (1,)), ((), ())),
                                       preferred_element_type=F32)
        for h in range(2):
            hs = slice(h * HEAD_PAD, (h + 1) * HEAD_PAD)
            for r in range(tq // rb):
                rows = slice(r * rb, (r + 1) * rb)
                mx = s_scr[h, rows, 0:LANES]
                for c in range(1, nc):
                    mx = jnp.maximum(mx, s_scr[h, rows, c * LANES:(c + 1) * LANES])
                m_old = m_scr[h, rows, :]
                m_new = jnp.maximum(m_old, jnp.max(mx, axis=-1, keepdims=True))
                m_scr[h, rows, :] = m_new
                acc_scr[h, rows, :] = acc_scr[h, rows, :] * jnp.exp2(m_old - m_new)
                for c in range(nc):
                    cs = slice(c * LANES, (c + 1) * LANES)
                    p_scr[h, rows, cs] = jnp.exp2(s_scr[h, rows, cs] - m_new).astype(BF16)
            acc_scr[h] += jnp.dot(p_scr[h], v_ref[pl.ds(off, tk), hs], preferred_element_type=F32)
        return carry

    lax.fori_loop(0, nk, step, 0, unroll=2)
    a0, a1 = acc_scr[0], acc_scr[1]
    lane = lax.broadcasted_iota(jnp.int32, a0.shape, 1)
    o_ref[...] = jnp.where(lane < V_DIM, a0 / a0[:, V_DIM:V_DIM + 1], a1 / a1[:, 0:1])


def _attn(q, k, v, tq, tk, rb):
    B, L, _ = q.shape
    pair = 2 * HEAD_PAD
    return pl.pallas_call(
        functools.partial(_attn_body, tk=tk, rb=rb),
        grid=(B, N_HEADS // 2, L // tq),
        in_specs=[pl.BlockSpec((None, tq, pair), lambda b, h, i: (b, i, h)),
                  pl.BlockSpec((None, L, pair), lambda b, h, i: (b, 0, h)),
                  pl.BlockSpec((None, L, pair), lambda b, h, i: (b, 0, h))],
        out_specs=pl.BlockSpec((None, tq, 2 * V_DIM), lambda b, h, i: (b, i, h)),
        out_shape=jax.ShapeDtypeStruct((B, L, N_HEADS * V_DIM), F32),
        scratch_shapes=[pltpu.VMEM((2, tq, tk), F32), pltpu.VMEM((2, tq, tk), BF16),
                        pltpu.VMEM((2, tq, HEAD_PAD), F32), pltpu.VMEM((2, tq, HEAD_PAD), F32)],
        compiler_params=_params("parallel", "parallel", "parallel"),
        name="attn",
    )(q, k, v)


def _tail_body(x_ref, yh_ref, om_ref, p_ref, hnw_ref, mnw_ref, wo_ref, mlw_ref, w1_ref, w2_ref, wg_ref,
               wp_ref, pnw_ref, o_ref, *, fc):
    dot = functools.partial(jnp.dot, preferred_element_type=F32)
    a = _rms(yh_ref[...], hnw_ref[...]).astype(BF16)
    b = _rms(om_ref[...], mnw_ref[...]).astype(BF16)
    x = x_ref[...] + dot(a, wo_ref[:HY_WIDTH, :]) + dot(b, wo_ref[HY_WIDTH:, :])
    xn = _rms(x, mlw_ref[...]).astype(BF16)
    mlp = jnp.zeros_like(x)
    for c in range(w1_ref.shape[1] // fc):
        h = jnp.maximum(dot(xn, w1_ref[:, c * fc:(c + 1) * fc]), 0.0)
        mlp = mlp + dot((h * h).astype(BF16), w2_ref[c * fc:(c + 1) * fc, :])
    x = x + mlp
    gate = jax.nn.sigmoid(dot(x.astype(BF16), wg_ref[...]))
    pp = _rms(dot(p_ref[...].astype(BF16), wp_ref[...]), pnw_ref[...])
    o_ref[...] = x + gate * pp


def _tail(x, yh, om, p, hnw, mnw, wo, mlw, w1, w2, wg, wp, pnw, tm, fc):
    B, L, D = x.shape
    tok = lambda w: pl.BlockSpec((None, tm, w), lambda b, t: (b, t, 0))
    res = lambda a: pl.BlockSpec(a.shape, lambda b, t: (0,) * a.ndim, pipeline_mode=pl.Buffered(1))
    return pl.pallas_call(
        functools.partial(_tail_body, fc=fc),
        grid=(B, L // tm),
        in_specs=[tok(D), tok(yh.shape[-1]), tok(om.shape[-1]), tok(p.shape[-1]),
                  res(hnw), res(mnw), res(wo), res(mlw), res(w1), res(w2), res(wg), res(wp), res(pnw)],
        out_specs=tok(D),
        out_shape=jax.ShapeDtypeStruct((B, L, D), F32),
        compiler_params=_params("parallel", "parallel"),
        name="tail",
    )(x, yh, om, p, hnw, mnw, wo, mlw, w1, w2, wg, wp, pnw)


def _rope_tables(L):
    inv = ROPE_THETA ** (-(2.0 * jnp.arange(HALF_ROPE, dtype=F32)) / QK_ROPE)
    ang = jnp.arange(L, dtype=F32)[:, None] * inv[None, :]
    cos, sin = jnp.cos(ang), jnp.sin(ang)
    one = jnp.ones((L, QK_NOPE), F32)
    pad = jnp.zeros((L, HEAD_PAD - QK_DIM), F32)
    cosf = jnp.concatenate([one, cos, cos, pad], axis=1)
    sinf = jnp.concatenate([0.0 * one, -sin, sin, pad], axis=1)
    return cosf, sinf


def _filter_features(L):
    pos = jnp.concatenate([jnp.arange(L), jnp.arange(L - 1, -1, -1)])
    t = jnp.linspace(0.0, 1.0, L, dtype=F32)[pos][:, None]
    w = (2.0 * math.pi) * pos.astype(F32)[:, None] / L
    f = jnp.linspace(1e-4, FILT_BANDS - 1, FILT_BANDS, dtype=F32)[None, :]
    z = jnp.concatenate([t, jnp.cos(f * w), -jnp.sin(f * w)], axis=-1)
    return jnp.pad(z, ((0, 0), (0, LANES - z.shape[1])))


def _decay_rates():
    min_decay = math.log(DECAY_TARGET) / FAST_DECAY_PCT
    max_decay = math.log(DECAY_TARGET) / SLOW_DECAY_PCT
    return jnp.abs(jnp.linspace(min_decay, max_decay, HY_WIDTH, dtype=F32))[None, :]


def _pad_heads(w, width, lead=0):
    w = w.reshape(w.shape[:-1] + (N_HEADS, width))
    pad = [(0, 0)] * (w.ndim - 1) + [(lead, HEAD_PAD - width - lead)]
    return jnp.pad(w, pad).reshape(w.shape[:-2] + (N_HEADS * HEAD_PAD,))


def _rope_partner(w):
    a, b = w[..., QK_NOPE:QK_NOPE + HALF_ROPE], w[..., QK_NOPE + HALF_ROPE:]
    zeros = lambda n: jnp.zeros(w.shape[:-1] + (n,), w.dtype)
    return jnp.concatenate([zeros(QK_NOPE), b, a, zeros(HEAD_PAD - QK_DIM)], axis=-1)


def _layer_weights(attn_norm_w, w_in, q_a_norm_w, w_q_b, kv_a_norm_w, w_kv_b, q_norm_w, k_norm_w):
    d = w_in.shape[0]
    o3 = HY3 + Q_LORA + KV_LORA
    rope_cols = jnp.pad(w_in[:, o3:], ((0, 0), (QK_NOPE, 0)))
    win = jnp.concatenate([w_in[:, :o3], jnp.pad(rope_cols, ((0, 0), (0, HEAD_PAD - QK_DIM))),
                           _rope_partner(rope_cols)], axis=1).astype(BF16)
    wq = w_q_b.reshape(Q_LORA, N_HEADS, QK_DIM)
    wqb = jnp.concatenate([_pad_heads(w_q_b, QK_DIM), _rope_partner(wq).reshape(Q_LORA, N_HEADS * HEAD_PAD)],
                          axis=1).astype(BF16)
    kv = w_kv_b.reshape(KV_LORA, N_HEADS, QK_NOPE + V_DIM)
    wk = _pad_heads(kv[:, :, :QK_NOPE].reshape(KV_LORA, N_HEADS * QK_NOPE), QK_NOPE)
    v4 = kv[:, :, QK_NOPE:].reshape(KV_LORA, N_HEADS // 2, 2, V_DIM)
    zero = jnp.zeros_like(v4[:, :, 0])
    wv = jnp.stack([v4[:, :, 0], zero, zero, v4[:, :, 1]], axis=2).reshape(KV_LORA, N_HEADS * HEAD_PAD)
    wkvb = jnp.concatenate([wk, wv], axis=1).astype(BF16)
    norm2 = lambda w: jnp.stack([jnp.pad(w, (0, HEAD_PAD - QK_DIM)), _rope_partner(w)])
    return (attn_norm_w[None, :], win, q_a_norm_w[None, :], wqb, kv_a_norm_w[None, :], wkvb,
            norm2(q_norm_w), norm2(k_norm_w))


def _group(x, p, lw, conv_w, conv_b, filt, filt_bias, tailw):
    B, L, D = x.shape
    cosf, sinf = _rope_tables(L)
    x0, z, q, k, v = _inproj(x, lw[0], lw[1], conv_w, conv_b, *lw[2:], cosf, sinf, tm=512)
    kern = _filt(_filter_features(L), *filt, _decay_rates(), tf=512)
    yh = _hyena_conv(x0, z, kern, filt_bias, kb=8, rb=16, cw=HY_WIDTH)
    om = _attn(q, k, v, tq=min(1024, L), tk=min(1024, L), rb=128)
    return _tail(x, yh, om, p, *tailw, tm=512, fc=1024)


def kernel(x_prompt, x_sample, p_prompt, p_sample, attn_norm_w, w_in, conv_w, conv_b, filt_w1, filt_b1, filt_freq, filt_w2, filt_b2, filt_w3, filt_bias, q_a_norm_w, w_q_b, kv_a_norm_w, w_kv_b, q_norm_w, k_norm_w, hy_out_norm_w, mla_out_norm_w, w_out, mlp_norm_w, w_mlp1, w_mlp2, w_ple_gate, w_ple_proj, ple_norm_w):
    depth = w_in.shape[0]
    for i in range(depth):
        lw = _layer_weights(attn_norm_w[i], w_in[i], q_a_norm_w[i], w_q_b[i], kv_a_norm_w[i], w_kv_b[i],
                            q_norm_w[i], k_norm_w[i])
        w1 = jnp.pad(filt_w1[i], ((0, LANES - filt_w1.shape[1]), (0, 0)))
        filt = (w1, filt_b1[i][None, :], filt_freq[i][None, :], filt_w2[i], filt_b2[i][None, :], filt_w3[i])
        tailw = (hy_out_norm_w[i][None, :], mla_out_norm_w[i][None, :], w_out[i].astype(BF16),
                 mlp_norm_w[i][None, :], w_mlp1[i].astype(BF16), w_mlp2[i].astype(BF16),
                 w_ple_gate[i].astype(BF16), w_ple_proj[i].astype(BF16), ple_norm_w[i][None, :])
        x_prompt = _group(x_prompt, p_prompt[i], lw, conv_w[i], conv_b[i][None, :], filt, filt_bias[i], tailw)
        x_sample = _group(x_sample, p_sample[i], lw, conv_w[i], conv_b[i][None, :], filt, filt_bias[i], tailw)
    return (x_prompt, x_sample)
```

```python
import functools
import math

import jax
import jax.numpy as jnp
from jax import lax
from jax.experimental import pallas as pl
from jax.experimental.pallas import tpu as pltpu

F32 = jnp.float32
BF16 = jnp.bfloat16
HIGHEST = lax.Precision.HIGHEST

LANES = 128
SUBLANES = 8
VMEM_LIMIT = 56 * 1024 * 1024

HY_WIDTH = 512
HY_STREAMS = 3
HY3 = HY_STREAMS * HY_WIDTH
FILT_BANDS = 16
FILT_HIDDEN = 64
DECAY_TARGET = 1e-2
FAST_DECAY_PCT = 0.3
SLOW_DECAY_PCT = 1.5
N_HEADS = 8
QK_NOPE = 64
QK_ROPE = 32
QK_DIM = QK_NOPE + QK_ROPE
V_DIM = 64
Q_LORA = 256
KV_LORA = 128
ROPE_THETA = 10000.0
EPS = 1e-6
HEAD_PAD = LANES
HALF_ROPE = QK_ROPE // 2


def _rms(x, w):
    ms = jnp.mean(x * x, axis=-1, keepdims=True)
    return x * lax.rsqrt(ms + EPS) * w


def _params(*sem):
    return pltpu.CompilerParams(dimension_semantics=sem, vmem_limit_bytes=VMEM_LIMIT)


def _const_spec(shape):
    nd = len(shape)
    return pl.BlockSpec(shape, lambda *_: (0,) * nd)


def _inproj_body(xp_ref, x_ref, xn_ref, anw_ref, win_ref, cw_ref, cb_ref, qaw_ref, wqb_ref, kvaw_ref, wkvb_ref,
                 qnw_ref, knw_ref, cos_ref, sin_ref, x0_ref, z_ref, q_ref, k_ref, v_ref):
    t = pl.program_id(1)
    tm = x_ref.shape[0]
    has_prev = (t > 0).astype(F32)
    has_next = (t < pl.num_programs(1) - 1).astype(F32)
    xa = jnp.concatenate([xp_ref[...] * has_prev, x_ref[...], xn_ref[...] * has_next], axis=0)
    proj = jnp.dot(_rms(xa, anw_ref[...]).astype(BF16), win_ref[...], preferred_element_type=F32)
    rows = xa.shape[0]
    body = slice(SUBLANES, SUBLANES + tm)

    def stream(s):
        sl = slice(s * HY_WIDTH, (s + 1) * HY_WIDTH)
        h = proj[:, sl]
        dn, up = pltpu.roll(h, 1, 0)[body], pltpu.roll(h, rows - 1, 0)[body]
        return cb_ref[:, sl] + dn * cw_ref[0:1, sl] + h[body] * cw_ref[1:2, sl] + up * cw_ref[2:3, sl]

    x0_ref[...] = stream(0)
    z_ref[...] = stream(2) * stream(1)

    o2 = HY3 + Q_LORA
    o3 = o2 + KV_LORA
    o4 = o3 + HEAD_PAD
    q_lat, kv_lat, krope, krope_pi = proj[body, HY3:o2], proj[body, o2:o3], proj[body, o3:o4], proj[body, o4:]
    qf = jnp.dot(_rms(q_lat, qaw_ref[...]).astype(BF16), wqb_ref[...], preferred_element_type=F32)
    kvf = jnp.dot(_rms(kv_lat, kvaw_ref[...]).astype(BF16), wkvb_ref[...], preferred_element_type=F32)
    kw = N_HEADS * HEAD_PAD
    lane = lax.broadcasted_iota(jnp.int32, (1, kw), 1) % (2 * HEAD_PAD)
    ones_col = (lane == V_DIM) | (lane == HEAD_PAD)
    v_ref[...] = jnp.where(ones_col, 1.0, kvf[:, kw:]).astype(BF16)

    cosf, sinf = cos_ref[...], sin_ref[...]
    qscale = QK_DIM ** -0.5 * math.log2(math.e)
    cq, sq = cosf * (qnw_ref[0:1] * qscale), sinf * (qnw_ref[1:2] * qscale)
    ck, sk = cosf * knw_ref[0:1], sinf * knw_ref[1:2]
    k_pi = krope_pi * sk

    def rstd(xh):
        return lax.rsqrt(jnp.sum(xh * xh, axis=-1, keepdims=True) / QK_DIM + EPS)

    for h in range(N_HEADS):
        sl = slice(h * HEAD_PAD, (h + 1) * HEAD_PAD)
        xq = qf[:, sl]
        q_ref[:, sl] = ((xq * cq + qf[:, kw + h * HEAD_PAD:kw + (h + 1) * HEAD_PAD] * sq) * rstd(xq)).astype(BF16)
        xk = kvf[:, sl] + krope
        k_ref[:, sl] = ((xk * ck + k_pi) * rstd(xk)).astype(BF16)


def _inproj(x, anw, win, conv_w, conv_b, qaw, wqb, kvaw, wkvb, qnw, knw, cosf, sinf, tm):
    B, L, D = x.shape
    kw = N_HEADS * HEAD_PAD
    nb = tm // SUBLANES
    last = L // SUBLANES - 1
    tok = lambda w: pl.BlockSpec((None, tm, w), lambda b, t: (b, t, 0))
    pos = pl.BlockSpec((tm, HEAD_PAD), lambda b, t: (t, 0))
    consts = [anw, win, conv_w, conv_b, qaw, wqb, kvaw, wkvb, qnw, knw]
    return pl.pallas_call(
        _inproj_body,
        grid=(B, L // tm),
        in_specs=[pl.BlockSpec((None, SUBLANES, D), lambda b, t: (b, jnp.maximum(t * nb - 1, 0), 0)), tok(D),
                  pl.BlockSpec((None, SUBLANES, D), lambda b, t: (b, jnp.minimum((t + 1) * nb, last), 0))]
                 + [_const_spec(c.shape) for c in consts] + [pos, pos],
        out_specs=[tok(HY_WIDTH), tok(HY_WIDTH), tok(kw), tok(kw), tok(kw)],
        out_shape=[jax.ShapeDtypeStruct((B, L, HY_WIDTH), F32)] * 2 + [jax.ShapeDtypeStruct((B, L, kw), BF16)] * 3,
        compiler_params=_params("parallel", "parallel"),
        name="inproj",
    )(x, x, x, *consts, cosf, sinf)


def _filt_body(z_ref, w1_ref, b1_ref, fr_ref, w2_ref, b2_ref, w3_ref, dl_ref, o_ref):
    hr = z_ref.shape[0] // 2
    top, bot = z_ref[:hr], z_ref[hr:]
    fr = fr_ref[...]
    dot = functools.partial(jnp.dot, precision=HIGHEST, preferred_element_type=F32)
    h = jnp.sin(fr * (dot(jnp.concatenate([top, bot], axis=1), w1_ref[...]) + b1_ref[...]))
    h = jnp.sin(fr * (dot(h, w2_ref[...]) + b2_ref[...]))
    h = dot(h, w3_ref[...])
    dl = dl_ref[...]
    o_ref[:hr] = h[:, :HY_WIDTH] * jnp.exp(-top[:, 0:1] * dl)
    o_ref[hr:] = h[:, HY_WIDTH:] * jnp.exp(-bot[:, 0:1] * dl)


def _filt(feat, w1, b1, fr, w2, b2, w3, deltas, tf):
    n2l = feat.shape[0]
    half = n2l // tf // 2
    return pl.pallas_call(
        _filt_body,
        grid=(n2l // tf,),
        in_specs=[pl.BlockSpec((tf, LANES), lambda t: (t, 0)), _const_spec(w1.shape), _const_spec(b1.shape),
                  _const_spec(fr.shape), _const_spec(w2.shape), _const_spec(b2.shape),
                  pl.BlockSpec((None,) + w3.shape[1:], lambda t: (t // half, 0, 0)),
                  _const_spec(deltas.shape)],
        out_specs=pl.BlockSpec((tf, HY_WIDTH), lambda t: (t, 0)),
        out_shape=jax.ShapeDtypeStruct((n2l, HY_WIDTH), F32),
        compiler_params=_params("parallel"),
        name="filt",
    )(feat, w1, b1, fr, w2, b2, w3, deltas)


def _hi_lo(x):
    hi = x.astype(BF16)
    return hi, (x - hi.astype(F32)).astype(BF16)


def _dot3(m_ref, x):
    xh, xl = _hi_lo(x)
    mh = m_ref[0]
    dot = functools.partial(jnp.dot, preferred_element_type=F32)
    return dot(mh, xh) + dot(mh, xl) + dot(m_ref[1], xh)


def _dft1_body(c_ref, x_ref, o_ref):
    k, rb, cw = x_ref.shape
    m = o_ref.shape[0]
    x2 = x_ref.reshape(k * rb, cw)
    o2 = o_ref.reshape(m * rb, cw)
    for s in range(rb):
        o2[pl.ds(s, m, stride=rb), :] = _dot3(c_ref, x2[pl.ds(s, k, stride=rb), :])


def _dft1(cmat, xv, rb, cw):
    P, K, n2, C = xv.shape
    M = cmat.shape[1]
    return pl.pallas_call(
        _dft1_body,
        grid=(P, n2 // rb, C // cw),
        in_specs=[_const_spec(cmat.shape), pl.BlockSpec((None, K, rb, cw), lambda p, i, c: (p, 0, i, c))],
        out_specs=pl.BlockSpec((None, M, rb, cw), lambda p, i, c: (p, 0, i, c)),
        out_shape=jax.ShapeDtypeStruct((P, M, n2, C), F32),
        compiler_params=_params("parallel", "parallel", "parallel"),
        name="dft1",
    )(cmat, xv)


def _cmul(ar, ai, br, bi):
    return ar * br - ai * bi, ar * bi + ai * br


def _fwd2(f_ref, tw_ref, a_ref, k):
    n2, cw = a_ref.shape[2:]
    reps = cw // LANES
    twr = jnp.concatenate([tw_ref[0, k]] * reps, axis=1)
    twi = jnp.concatenate([tw_ref[1, k]] * reps, axis=1)
    ar, ai = _cmul(a_ref[0, k], a_ref[1, k], twr, twi)
    x = _dot3(f_ref, jnp.concatenate([ar, ai], axis=0))
    return x[:n2], x[n2:], twr, twi


def _mid_kernel_body(f_ref, tw_ref, a_ref, o_ref):
    for k in range(a_ref.shape[1]):
        o_ref[0, k], o_ref[1, k], _, _ = _fwd2(f_ref, tw_ref, a_ref, k)


def _mid_body(f_ref, ft_ref, tw_ref, kf_ref, a_ref, o_ref):
    n2 = a_ref.shape[2]
    for k in range(a_ref.shape[1]):
        xr, xi, twr, twi = _fwd2(f_ref, tw_ref, a_ref, k)
        yr, yi = _cmul(xr, xi, kf_ref[0, k], kf_ref[1, k])
        b = _dot3(ft_ref, jnp.concatenate([yr, yi], axis=0))
        o_ref[0, k], o_ref[1, k] = _cmul(b[:n2], b[n2:], twr, -twi)


def _mid(f2, f2_t, tw, kf, a, kb, cw):
    P, _, n1, n2, C = a.shape
    dat = pl.BlockSpec((None, 2, kb, n2, cw), lambda k, c, p: (p, 0, k, 0, c))
    tws = pl.BlockSpec((2, kb, n2, LANES), lambda k, c, p: (0, k, 0, 0))
    mat = _const_spec(f2.shape)
    if kf is None:
        body, ins, specs = _mid_kernel_body, (f2, tw, a), [mat, tws, dat]
    else:
        kspec = pl.BlockSpec((None, 2, kb, n2, cw), lambda k, c, p: (0, 0, k, 0, c))
        body, ins, specs = _mid_body, (f2, f2_t, tw, kf, a), [mat, mat, tws, kspec, dat]
    return pl.pallas_call(
        body,
        grid=(n1 // kb, C // cw, P),
        in_specs=specs,
        out_specs=dat,
        out_shape=jax.ShapeDtypeStruct(a.shape, F32),
        compiler_params=_params("parallel", "parallel", "parallel"),
        name="mid",
    )(*ins)


def _dft4_body(c_ref, b_ref, x0_ref, z_ref, bias_ref, o_ref):
    k, rb, cw = b_ref.shape
    m = o_ref.shape[0]
    flat = lambda r: r.reshape(r.shape[0] * rb, cw)
    b2, x02, z2, o2 = flat(b_ref), flat(x0_ref), flat(z_ref), flat(o_ref)
    bias = bias_ref[...]
    for s in range(rb):
        rows = pl.ds(s, m, stride=rb)
        y = _dot3(c_ref, b2[pl.ds(s, k, stride=rb), :])
        o2[rows, :] = x02[rows, :] * (y + z2[rows, :] * bias)


def _dft4(cmat, bv, x0v, zv, bias, rb, cw):
    P, K, n2, C = bv.shape
    M = cmat.shape[1]
    blk = lambda r: pl.BlockSpec((None, r, rb, cw), lambda p, i, c: (p, 0, i, c))
    return pl.pallas_call(
        _dft4_body,
        grid=(P, n2 // rb, C // cw),
        in_specs=[_const_spec(cmat.shape), blk(K), blk(M), blk(M), pl.BlockSpec((1, cw), lambda p, i, c: (0, c))],
        out_specs=blk(M),
        out_shape=jax.ShapeDtypeStruct((P, M, n2, C), F32),
        compiler_params=_params("parallel", "parallel", "parallel"),
        name="dft4",
    )(cmat, bv, x0v, zv, bias)


def _dft_tables(n1, n2):
    n = n1 * n2
    h = n1 // 2
    i1 = jnp.arange(n1, dtype=jnp.int32)
    th = (2.0 * math.pi / n1) * ((i1[:, None] * i1[None, :]) % n1).astype(F32)
    c1, s1 = jnp.cos(th), jnp.sin(th)
    cz = jnp.block([[c1[:, :h], s1[:, :h]], [-s1[:, :h], c1[:, :h]]])
    ck = jnp.concatenate([c1, -s1], axis=0)
    c4 = jnp.block([[c1[:h], -s1[:h]], [s1[:h], c1[:h]]]) * (1.0 / n)
    u = jnp.arange(n2, dtype=jnp.int32)
    ph2 = (2.0 * math.pi / n2) * ((u[:, None] * u[None, :]) % n2).astype(F32)
    c2, s2 = jnp.cos(ph2), jnp.sin(ph2)
    f2 = jnp.block([[c2, s2], [-s2, c2]])
    pht = (2.0 * math.pi / n) * (i1[:, None] * u[None, :]).astype(F32)
    tw = jnp.stack([jnp.cos(pht), -jnp.sin(pht)])
    tw = jnp.broadcast_to(tw[..., None], tw.shape + (LANES,))
    parts = lambda m: jnp.stack(_hi_lo(m))
    return parts(cz), parts(ck), parts(c4), parts(f2), parts(f2.T), tw


def _split(L):
    n = 2 * L
    n1 = 1 << ((n.bit_length() - 1) // 2)
    return n1, n // n1


def _hyena_conv(x0, z, kern, filt_bias, kb, rb, cw):
    B, L, C = z.shape
    n1, n2 = _split(L)
    P = B // 2
    cz, ck, c4, f2, f2_t, tw = _dft_tables(n1, n2)
    ka = _dft1(ck, kern.reshape(1, n1, n2, C), rb, LANES)
    kf = _mid(f2, None, tw, None, ka.reshape(1, 2, n1, n2, C), kb, cw)
    za = _dft1(cz, z.reshape(P, n1, n2, C), rb, LANES)
    zb = _mid(f2, f2_t, tw, kf, za.reshape(P, 2, n1, n2, C), kb, cw)
    y = _dft4(c4, zb.reshape(P, 2 * n1, n2, C), x0.reshape(P, n1, n2, C), z.reshape(P, n1, n2, C),
              filt_bias.reshape(1, C), rb, LANES)
    return y.reshape(B, L, C)


def _attn_body(q_ref, k_ref, v_ref, o_ref, s_scr, p_scr, m_scr, acc_scr, *, tk, rb):
    tq = q_ref.shape[0]
    nk = k_ref.shape[0] // tk
    nc = tk // LANES
    m_scr[...] = jnp.full(m_scr.shape, -jnp.inf, F32)
    acc_scr[...] = jnp.zeros(acc_scr.shape, F32)

    def step(j, carry):
        off = pl.multiple_of(j * tk, tk)
        for h in range(2):
            hs = slice(h * HEAD_PAD, (h + 1) * HEAD_PAD)
            s_scr[h] = lax.dot_general(q_ref[:, hs], k_ref[pl.ds(off, tk), hs], (((1,), (1,)), ((), ())),
                                       preferred_element_type=F32)
        for h in range(2):
            hs = slice(h * HEAD_PAD, (h + 1) * HEAD_PAD)
            for r in range(tq // rb):
                rows = slice(r * rb, (r + 1) * rb)
                mx = s_scr[h, rows, 0:LANES]
                for c in range(1, nc):
                    mx = jnp.maximum(mx, s_scr[h, rows, c * LANES:(c + 1) * LANES])
                m_old = m_scr[h, rows, :]
                m_new = jnp.maximum(m_old, jnp.max(mx, axis=-1, keepdims=True))
                m_scr[h, rows, :] = m_new
                acc_scr[h, rows, :] = acc_scr[h, rows, :] * jnp.exp2(m_old - m_new)
                for c in range(nc):
                    cs = slice(c * LANES, (c + 1) * LANES)
                    p_scr[h, rows, cs] = jnp.exp2(s_scr[h, rows, cs] - m_new).astype(BF16)
            acc_scr[h] += jnp.dot(p_scr[h], v_ref[pl.ds(off, tk), hs], preferred_element_type=F32)
        return carry

    lax.fori_loop(0, nk, step, 0, unroll=2)
    a0, a1 = acc_scr[0], acc_scr[1]
    lane = lax.broadcasted_iota(jnp.int32, a0.shape, 1)
    o_ref[...] = jnp.where(lane < V_DIM, a0 / a0[:, V_DIM:V_DIM + 1], a1 / a1[:, 0:1])


def _attn(q, k, v, tq, tk, rb):
    B, L, _ = q.shape
    pair = 2 * HEAD_PAD
    return pl.pallas_call(
        functools.partial(_attn_body, tk=tk, rb=rb),
        grid=(B, N_HEADS // 2, L // tq),
        in_specs=[pl.BlockSpec((None, tq, pair), lambda b, h, i: (b, i, h)),
                  pl.BlockSpec((None, L, pair), lambda b, h, i: (b, 0, h)),
                  pl.BlockSpec((None, L, pair), lambda b, h, i: (b, 0, h))],
        out_specs=pl.BlockSpec((None, tq, 2 * V_DIM), lambda b, h, i: (b, i, h)),
        out_shape=jax.ShapeDtypeStruct((B, L, N_HEADS * V_DIM), F32),
        scratch_shapes=[pltpu.VMEM((2, tq, tk), F32), pltpu.VMEM((2, tq, tk), BF16),
                        pltpu.VMEM((2, tq, HEAD_PAD), F32), pltpu.VMEM((2, tq, HEAD_PAD), F32)],
        compiler_params=_params("parallel", "parallel", "parallel"),
        name="attn",
    )(q, k, v)


def _tail_body(x_ref, yh_ref, om_ref, p_ref, hnw_ref, mnw_ref, wo_ref, mlw_ref, w1_ref, w2_ref, wg_ref,
               wp_ref, pnw_ref, o_ref, *, fc):
    dot = functools.partial(jnp.dot, preferred_element_type=F32)
    a = _rms(yh_ref[...], hnw_ref[...]).astype(BF16)
    b = _rms(om_ref[...], mnw_ref[...]).astype(BF16)
    x = x_ref[...] + dot(a, wo_ref[:HY_WIDTH, :]) + dot(b, wo_ref[HY_WIDTH:, :])
    xn = _rms(x, mlw_ref[...]).astype(BF16)
    mlp = jnp.zeros_like(x)
    for c in range(w1_ref.shape[1] // fc):
        h = jnp.maximum(dot(xn, w1_ref[:, c * fc:(c + 1) * fc]), 0.0)
        mlp = mlp + dot((h * h).astype(BF16), w2_ref[c * fc:(c + 1) * fc, :])
    x = x + mlp
    gate = jax.nn.sigmoid(dot(x.astype(BF16), wg_ref[...]))
    pp = _rms(dot(p_ref[...].astype(BF16), wp_ref[...]), pnw_ref[...])
    o_ref[...] = x + gate * pp


def _tail(x, yh, om, p, hnw, mnw, wo, mlw, w1, w2, wg, wp, pnw, tm, fc):
    B, L, D = x.shape
    tok = lambda w: pl.BlockSpec((None, tm, w), lambda b, t: (b, t, 0))
    res = lambda a: pl.BlockSpec(a.shape, lambda b, t: (0,) * a.ndim, pipeline_mode=pl.Buffered(1))
    return pl.pallas_call(
        functools.partial(_tail_body, fc=fc),
        grid=(B, L // tm),
        in_specs=[tok(D), tok(yh.shape[-1]), tok(om.shape[-1]), tok(p.shape[-1]),
                  res(hnw), res(mnw), res(wo), res(mlw), res(w1), res(w2), res(wg), res(wp), res(pnw)],
        out_specs=tok(D),
        out_shape=jax.ShapeDtypeStruct((B, L, D), F32),
        compiler_params=_params("parallel", "parallel"),
        name="tail",
    )(x, yh, om, p, hnw, mnw, wo, mlw, w1, w2, wg, wp, pnw)


def _rope_tables(L):
    inv = ROPE_THETA ** (-(2.0 * jnp.arange(HALF_ROPE, dtype=F32)) / QK_ROPE)
    ang = jnp.arange(L, dtype=F32)[:, None] * inv[None, :]
    cos, sin = jnp.cos(ang), jnp.sin(ang)
    one = jnp.ones((L, QK_NOPE), F32)
    pad = jnp.zeros((L, HEAD_PAD - QK_DIM), F32)
    cosf = jnp.concatenate([one, cos, cos, pad], axis=1)
    sinf = jnp.concatenate([0.0 * one, -sin, sin, pad], axis=1)
    return cosf, sinf


def _pair_weights(w):
    z = jnp.zeros_like(w)
    return jnp.block([[w, z], [z, w]])


def _filter_features(L):
    pos = jnp.concatenate([jnp.arange(L), jnp.arange(L - 1, -1, -1)]).astype(F32)[:, None]
    t = pos / (L - 1)
    w = (2.0 * math.pi) * pos / L
    f = jnp.linspace(1e-4, FILT_BANDS - 1, FILT_BANDS, dtype=F32)[None, :]
    z = jnp.concatenate([t, jnp.cos(f * w), -jnp.sin(f * w)], axis=-1)
    return jnp.pad(z, ((0, 0), (0, LANES - z.shape[1])))


def _decay_rates():
    min_decay = math.log(DECAY_TARGET) / FAST_DECAY_PCT
    max_decay = math.log(DECAY_TARGET) / SLOW_DECAY_PCT
    return jnp.abs(jnp.linspace(min_decay, max_decay, HY_WIDTH, dtype=F32))[None, :]


def _pad_heads(w, width, lead=0):
    w = w.reshape(w.shape[:-1] + (N_HEADS, width))
    pad = [(0, 0)] * (w.ndim - 1) + [(lead, HEAD_PAD - width - lead)]
    return jnp.pad(w, pad).reshape(w.shape[:-2] + (N_HEADS * HEAD_PAD,))


def _rope_partner(w):
    a, b = w[..., QK_NOPE:QK_NOPE + HALF_ROPE], w[..., QK_NOPE + HALF_ROPE:]
    zeros = lambda n: jnp.zeros(w.shape[:-1] + (n,), w.dtype)
    return jnp.concatenate([zeros(QK_NOPE), b, a, zeros(HEAD_PAD - QK_DIM)], axis=-1)


def _layer_weights(attn_norm_w, w_in, q_a_norm_w, w_q_b, kv_a_norm_w, w_kv_b, q_norm_w, k_norm_w):
    d = w_in.shape[0]
    o3 = HY3 + Q_LORA + KV_LORA
    rope_cols = jnp.pad(w_in[:, o3:], ((0, 0), (QK_NOPE, 0)))
    win = jnp.concatenate([w_in[:, :o3], jnp.pad(rope_cols, ((0, 0), (0, HEAD_PAD - QK_DIM))),
                           _rope_partner(rope_cols)], axis=1).astype(BF16)
    wq = w_q_b.reshape(Q_LORA, N_HEADS, QK_DIM)
    wqb = jnp.concatenate([_pad_heads(w_q_b, QK_DIM), _rope_partner(wq).reshape(Q_LORA, N_HEADS * HEAD_PAD)],
                          axis=1).astype(BF16)
    kv = w_kv_b.reshape(KV_LORA, N_HEADS, QK_NOPE + V_DIM)
    wk = _pad_heads(kv[:, :, :QK_NOPE].reshape(KV_LORA, N_HEADS * QK_NOPE), QK_NOPE)
    v4 = kv[:, :, QK_NOPE:].reshape(KV_LORA, N_HEADS // 2, 2, V_DIM)
    zero = jnp.zeros_like(v4[:, :, 0])
    wv = jnp.stack([v4[:, :, 0], zero, zero, v4[:, :, 1]], axis=2).reshape(KV_LORA, N_HEADS * HEAD_PAD)
    wkvb = jnp.concatenate([wk, wv], axis=1).astype(BF16)
    norm2 = lambda w: jnp.stack([jnp.pad(w, (0, HEAD_PAD - QK_DIM)), _rope_partner(w)])
    return (attn_norm_w[None, :], win, q_a_norm_w[None, :], wqb, kv_a_norm_w[None, :], wkvb,
            norm2(q_norm_w), norm2(k_norm_w))


def _group(x, p, lw, conv_w, conv_b, filt, filt_bias, tailw):
    B, L, D = x.shape
    cosf, sinf = _rope_tables(L)
    x0, z, q, k, v = _inproj(x, lw[0], lw[1], conv_w, conv_b, *lw[2:], cosf, sinf, tm=512)
    kern = _filt(_filter_features(L), *filt, _decay_rates(), tf=512)
    yh = _hyena_conv(x0, z, kern, filt_bias, kb=8, rb=16, cw=HY_WIDTH)
    om = _attn(q, k, v, tq=512, tk=min(1024, L), rb=128)
    return _tail(x, yh, om, p, *tailw, tm=512, fc=1024)


def kernel(x_prompt, x_sample, p_prompt, p_sample, attn_norm_w, w_in, conv_w, conv_b, filt_w1, filt_b1, filt_freq, filt_w2, filt_b2, filt_w3, filt_bias, q_a_norm_w, w_q_b, kv_a_norm_w, w_kv_b, q_norm_w, k_norm_w, hy_out_norm_w, mla_out_norm_w, w_out, mlp_norm_w, w_mlp1, w_mlp2, w_ple_gate, w_ple_proj, ple_norm_w):
    depth = w_in.shape[0]
    for i in range(depth):
        lw = _layer_weights(attn_norm_w[i], w_in[i], q_a_norm_w[i], w_q_b[i], kv_a_norm_w[i], w_kv_b[i],
                            q_norm_w[i], k_norm_w[i])
        w1 = jnp.pad(filt_w1[i], ((0, LANES - filt_w1.shape[1]), (0, 0)))
        twice = lambda v: jnp.concatenate([v, v])[None, :]
        w3 = jnp.stack([_pair_weights(filt_w3[i][:, :HY_WIDTH]), _pair_weights(filt_w3[i][:, HY_WIDTH:])])
        filt = (_pair_weights(w1), twice(filt_b1[i]), twice(filt_freq[i]), _pair_weights(filt_w2[i]),
                twice(filt_b2[i]), w3)
        tailw = (hy_out_norm_w[i][None, :], mla_out_norm_w[i][None, :], w_out[i].astype(BF16),
                 mlp_norm_w[i][None, :], w_mlp1[i].astype(BF16), w_mlp2[i].astype(BF16),
                 w_ple_gate[i].astype(BF16), w_ple_proj[i].astype(BF16), ple_norm_w[i][None, :])
        x_prompt = _group(x_prompt, p_prompt[i], lw, conv_w[i], conv_b[i][None, :], filt, filt_bias[i], tailw)
        x_sample = _group(x_sample, p_sample[i], lw, conv_w[i], conv_b[i][None, :], filt, filt_bias[i], tailw)
    return (x_prompt, x_sample)
```

```python
import functools
import math

import jax
import jax.numpy as jnp
from jax import lax
from jax.experimental import pallas as pl
from jax.experimental.pallas import tpu as pltpu

F32 = jnp.float32
BF16 = jnp.bfloat16
HIGHEST = lax.Precision.HIGHEST

LANES = 128
SUBLANES = 8
VMEM_LIMIT = 56 * 1024 * 1024

HY_WIDTH = 512
HY_STREAMS = 3
HY3 = HY_STREAMS * HY_WIDTH
FILT_BANDS = 16
FILT_HIDDEN = 64
DECAY_TARGET = 1e-2
FAST_DECAY_PCT = 0.3
SLOW_DECAY_PCT = 1.5
N_HEADS = 8
QK_NOPE = 64
QK_ROPE = 32
QK_DIM = QK_NOPE + QK_ROPE
V_DIM = 64
Q_LORA = 256
KV_LORA = 128
ROPE_THETA = 10000.0
EPS = 1e-6
HEAD_PAD = LANES
HALF_ROPE = QK_ROPE // 2


def _rms(x, w):
    ms = jnp.mean(x * x, axis=-1, keepdims=True)
    return x * lax.rsqrt(ms + EPS) * w


def _params(*sem):
    return pltpu.CompilerParams(dimension_semantics=sem, vmem_limit_bytes=VMEM_LIMIT)


def _const_spec(shape):
    nd = len(shape)
    return pl.BlockSpec(shape, lambda *_: (0,) * nd)


def _inproj_body(xp_ref, x_ref, xn_ref, anw_ref, win_ref, cw_ref, cb_ref, qaw_ref, wqb_ref, kvaw_ref, wkvb_ref,
                 qnw_ref, knw_ref, cos_ref, sin_ref, x0_ref, z_ref, q_ref, k_ref, v_ref):
    t = pl.program_id(1)
    tm = x_ref.shape[0]
    has_prev = (t > 0).astype(F32)
    has_next = (t < pl.num_programs(1) - 1).astype(F32)
    xa = jnp.concatenate([xp_ref[...] * has_prev, x_ref[...], xn_ref[...] * has_next], axis=0)
    proj = jnp.dot(_rms(xa, anw_ref[...]).astype(BF16), win_ref[...], preferred_element_type=F32)
    rows = xa.shape[0]
    body = slice(SUBLANES, SUBLANES + tm)

    def stream(s):
        sl = slice(s * HY_WIDTH, (s + 1) * HY_WIDTH)
        h = proj[:, sl]
        dn, up = pltpu.roll(h, 1, 0)[body], pltpu.roll(h, rows - 1, 0)[body]
        return cb_ref[:, sl] + dn * cw_ref[0:1, sl] + h[body] * cw_ref[1:2, sl] + up * cw_ref[2:3, sl]

    x0_ref[...] = stream(0)
    z_ref[...] = stream(2) * stream(1)

    o2 = HY3 + Q_LORA
    o3 = o2 + KV_LORA
    o4 = o3 + HEAD_PAD
    q_lat, kv_lat, krope, krope_pi = proj[body, HY3:o2], proj[body, o2:o3], proj[body, o3:o4], proj[body, o4:]
    qf = jnp.dot(_rms(q_lat, qaw_ref[...]).astype(BF16), wqb_ref[...], preferred_element_type=F32)
    kvf = jnp.dot(_rms(kv_lat, kvaw_ref[...]).astype(BF16), wkvb_ref[...], preferred_element_type=F32)
    kw = N_HEADS * HEAD_PAD
    lane = lax.broadcasted_iota(jnp.int32, (1, kw), 1) % (2 * HEAD_PAD)
    ones_col = (lane == V_DIM) | (lane == HEAD_PAD)
    v_ref[...] = jnp.where(ones_col, 1.0, kvf[:, kw:]).astype(BF16)

    cosf, sinf = cos_ref[...], sin_ref[...]
    qscale = QK_DIM ** -0.5 * math.log2(math.e)
    cq, sq = cosf * (qnw_ref[0:1] * qscale), sinf * (qnw_ref[1:2] * qscale)
    ck, sk = cosf * knw_ref[0:1], sinf * knw_ref[1:2]
    k_pi = krope_pi * sk

    def rstd(xh):
        return lax.rsqrt(jnp.sum(xh * xh, axis=-1, keepdims=True) / QK_DIM + EPS)

    for h in range(N_HEADS):
        sl = slice(h * HEAD_PAD, (h + 1) * HEAD_PAD)
        xq = qf[:, sl]
        q_ref[:, sl] = ((xq * cq + qf[:, kw + h * HEAD_PAD:kw + (h + 1) * HEAD_PAD] * sq) * rstd(xq)).astype(BF16)
        xk = kvf[:, sl] + krope
        k_ref[:, sl] = ((xk * ck + k_pi) * rstd(xk)).astype(BF16)


def _inproj(x, anw, win, conv_w, conv_b, qaw, wqb, kvaw, wkvb, qnw, knw, cosf, sinf, tm):
    B, L, D = x.shape
    kw = N_HEADS * HEAD_PAD
    nb = tm // SUBLANES
    last = L // SUBLANES - 1
    tok = lambda w: pl.BlockSpec((None, tm, w), lambda b, t: (b, t, 0))
    pos = pl.BlockSpec((tm, HEAD_PAD), lambda b, t: (t, 0))
    consts = [anw, win, conv_w, conv_b, qaw, wqb, kvaw, wkvb, qnw, knw]
    return pl.pallas_call(
        _inproj_body,
        grid=(B, L // tm),
        in_specs=[pl.BlockSpec((None, SUBLANES, D), lambda b, t: (b, jnp.maximum(t * nb - 1, 0), 0)), tok(D),
                  pl.BlockSpec((None, SUBLANES, D), lambda b, t: (b, jnp.minimum((t + 1) * nb, last), 0))]
                 + [_const_spec(c.shape) for c in consts] + [pos, pos],
        out_specs=[tok(HY_WIDTH), tok(HY_WIDTH), tok(kw), tok(kw), tok(kw)],
        out_shape=[jax.ShapeDtypeStruct((B, L, HY_WIDTH), F32)] * 2 + [jax.ShapeDtypeStruct((B, L, kw), BF16)] * 3,
        compiler_params=_params("parallel", "parallel"),
        name="inproj",
    )(x, x, x, *consts, cosf, sinf)


def _filt_body(z_ref, w1_ref, b1_ref, fr_ref, w2_ref, b2_ref, w3_ref, dl_ref, o_ref):
    hr = z_ref.shape[0] // 2
    top, bot = z_ref[:hr], z_ref[hr:]
    fr = fr_ref[...]
    dot = functools.partial(jnp.dot, precision=HIGHEST, preferred_element_type=F32)
    h = jnp.sin(fr * (dot(jnp.concatenate([top, bot], axis=1), w1_ref[...]) + b1_ref[...]))
    h = jnp.sin(fr * (dot(h, w2_ref[...]) + b2_ref[...]))
    h = dot(h, w3_ref[...])
    dl = dl_ref[...]
    o_ref[:hr] = h[:, :HY_WIDTH] * jnp.exp(-top[:, 0:1] * dl)
    o_ref[hr:] = h[:, HY_WIDTH:] * jnp.exp(-bot[:, 0:1] * dl)


def _filt(feat, w1, b1, fr, w2, b2, w3, deltas, tf):
    n2l = feat.shape[0]
    half = n2l // tf // 2
    return pl.pallas_call(
        _filt_body,
        grid=(n2l // tf,),
        in_specs=[pl.BlockSpec((tf, LANES), lambda t: (t, 0)), _const_spec(w1.shape), _const_spec(b1.shape),
                  _const_spec(fr.shape), _const_spec(w2.shape), _const_spec(b2.shape),
                  pl.BlockSpec((None,) + w3.shape[1:], lambda t: (t // half, 0, 0)),
                  _const_spec(deltas.shape)],
        out_specs=pl.BlockSpec((tf, HY_WIDTH), lambda t: (t, 0)),
        out_shape=jax.ShapeDtypeStruct((n2l, HY_WIDTH), F32),
        compiler_params=_params("parallel"),
        name="filt",
    )(feat, w1, b1, fr, w2, b2, w3, deltas)


def _hi_lo(x):
    hi = x.astype(BF16)
    return hi, (x - hi.astype(F32)).astype(BF16)


def _dot3(m_ref, x):
    xh, xl = _hi_lo(x)
    mh = m_ref[0]
    dot = functools.partial(jnp.dot, preferred_element_type=F32)
    return dot(mh, xh) + dot(mh, xl) + dot(m_ref[1], xh)


def _dft1_body(c_ref, x_ref, o_ref):
    k, rb, cw = x_ref.shape
    m = o_ref.shape[0]
    x2 = x_ref.reshape(k * rb, cw)
    o2 = o_ref.reshape(m * rb, cw)
    for s in range(rb):
        o2[pl.ds(s, m, stride=rb), :] = _dot3(c_ref, x2[pl.ds(s, k, stride=rb), :])


def _dft1(cmat, xv, rb, cw):
    P, K, n2, C = xv.shape
    M = cmat.shape[1]
    return pl.pallas_call(
        _dft1_body,
        grid=(P, n2 // rb, C // cw),
        in_specs=[_const_spec(cmat.shape), pl.BlockSpec((None, K, rb, cw), lambda p, i, c: (p, 0, i, c))],
        out_specs=pl.BlockSpec((None, M, rb, cw), lambda p, i, c: (p, 0, i, c)),
        out_shape=jax.ShapeDtypeStruct((P, M, n2, C), F32),
        compiler_params=_params("parallel", "parallel", "parallel"),
        name="dft1",
    )(cmat, xv)


def _cmul(ar, ai, br, bi):
    return ar * br - ai * bi, ar * bi + ai * br


def _fwd2(f_ref, tw_ref, a_ref, k):
    n2, cw = a_ref.shape[2:]
    reps = cw // LANES
    twr = jnp.concatenate([tw_ref[0, k]] * reps, axis=1)
    twi = jnp.concatenate([tw_ref[1, k]] * reps, axis=1)
    ar, ai = _cmul(a_ref[0, k], a_ref[1, k], twr, twi)
    x = _dot3(f_ref, jnp.concatenate([ar, ai], axis=0))
    return x[:n2], x[n2:], twr, twi


def _mid_kernel_body(f_ref, tw_ref, a_ref, o_ref):
    for k in range(a_ref.shape[1]):
        o_ref[0, k], o_ref[1, k], _, _ = _fwd2(f_ref, tw_ref, a_ref, k)


def _mid_body(f_ref, ft_ref, tw_ref, kf_ref, a_ref, o_ref):
    n2 = a_ref.shape[2]
    for k in range(a_ref.shape[1]):
        xr, xi, twr, twi = _fwd2(f_ref, tw_ref, a_ref, k)
        yr, yi = _cmul(xr, xi, kf_ref[0, k], kf_ref[1, k])
        b = _dot3(ft_ref, jnp.concatenate([yr, yi], axis=0))
        o_ref[0, k], o_ref[1, k] = _cmul(b[:n2], b[n2:], twr, -twi)


def _mid(f2, f2_t, tw, kf, a, kb, cw):
    P, _, n1, n2, C = a.shape
    dat = pl.BlockSpec((None, 2, kb, n2, cw), lambda k, c, p: (p, 0, k, 0, c))
    tws = pl.BlockSpec((2, kb, n2, LANES), lambda k, c, p: (0, k, 0, 0))
    mat = _const_spec(f2.shape)
    if kf is None:
        body, ins, specs = _mid_kernel_body, (f2, tw, a), [mat, tws, dat]
    else:
        kspec = pl.BlockSpec((None, 2, kb, n2, cw), lambda k, c, p: (0, 0, k, 0, c))
        body, ins, specs = _mid_body, (f2, f2_t, tw, kf, a), [mat, mat, tws, kspec, dat]
    return pl.pallas_call(
        body,
        grid=(n1 // kb, C // cw, P),
        in_specs=specs,
        out_specs=dat,
        out_shape=jax.ShapeDtypeStruct(a.shape, F32),
        compiler_params=_params("parallel", "parallel", "parallel"),
        name="mid",
    )(*ins)


def _dft4_body(c_ref, b_ref, x0_ref, z_ref, bias_ref, o_ref):
    k, rb, cw = b_ref.shape
    m = o_ref.shape[0]
    flat = lambda r: r.reshape(r.shape[0] * rb, cw)
    b2, x02, z2, o2 = flat(b_ref), flat(x0_ref), flat(z_ref), flat(o_ref)
    bias = bias_ref[...]
    for s in range(rb):
        rows = pl.ds(s, m, stride=rb)
        y = _dot3(c_ref, b2[pl.ds(s, k, stride=rb), :])
        o2[rows, :] = x02[rows, :] * (y + z2[rows, :] * bias)


def _dft4(cmat, bv, x0v, zv, bias, rb, cw):
    P, K, n2, C = bv.shape
    M = cmat.shape[1]
    blk = lambda r: pl.BlockSpec((None, r, rb, cw), lambda p, i, c: (p, 0, i, c))
    return pl.pallas_call(
        _dft4_body,
        grid=(P, n2 // rb, C // cw),
        in_specs=[_const_spec(cmat.shape), blk(K), blk(M), blk(M), pl.BlockSpec((1, cw), lambda p, i, c: (0, c))],
        out_specs=blk(M),
        out_shape=jax.ShapeDtypeStruct((P, M, n2, C), F32),
        compiler_params=_params("parallel", "parallel", "parallel"),
        name="dft4",
    )(cmat, bv, x0v, zv, bias)


def _dft_tables(n1, n2):
    n = n1 * n2
    h = n1 // 2
    i1 = jnp.arange(n1, dtype=jnp.int32)
    th = (2.0 * math.pi / n1) * ((i1[:, None] * i1[None, :]) % n1).astype(F32)
    c1, s1 = jnp.cos(th), jnp.sin(th)
    cz = jnp.block([[c1[:, :h], s1[:, :h]], [-s1[:, :h], c1[:, :h]]])
    ck = jnp.concatenate([c1, -s1], axis=0)
    c4 = jnp.block([[c1[:h], -s1[:h]], [s1[:h], c1[:h]]]) * (1.0 / n)
    u = jnp.arange(n2, dtype=jnp.int32)
    ph2 = (2.0 * math.pi / n2) * ((u[:, None] * u[None, :]) % n2).astype(F32)
    c2, s2 = jnp.cos(ph2), jnp.sin(ph2)
    f2 = jnp.block([[c2, s2], [-s2, c2]])
    pht = (2.0 * math.pi / n) * (i1[:, None] * u[None, :]).astype(F32)
    tw = jnp.stack([jnp.cos(pht), -jnp.sin(pht)])
    tw = jnp.broadcast_to(tw[..., None], tw.shape + (LANES,))
    parts = lambda m: jnp.stack(_hi_lo(m))
    return parts(cz), parts(ck), parts(c4), parts(f2), parts(f2.T), tw


def _split(L):
    n = 2 * L
    n1 = 1 << ((n.bit_length() - 1) // 2)
    return n1, n // n1


def _hyena_conv(x0, z, kern, filt_bias, kb, rb, cw):
    B, L, C = z.shape
    n1, n2 = _split(L)
    P = B // 2
    cz, ck, c4, f2, f2_t, tw = _dft_tables(n1, n2)
    ka = _dft1(ck, kern.reshape(1, n1, n2, C), rb, LANES)
    kf = _mid(f2, None, tw, None, ka.reshape(1, 2, n1, n2, C), kb, cw)
    za = _dft1(cz, z.reshape(P, n1, n2, C), rb, LANES)
    zb = _mid(f2, f2_t, tw, kf, za.reshape(P, 2, n1, n2, C), kb, cw)
    y = _dft4(c4, zb.reshape(P, 2 * n1, n2, C), x0.reshape(P, n1, n2, C), z.reshape(P, n1, n2, C),
              filt_bias.reshape(1, C), rb, LANES)
    return y.reshape(B, L, C)


def _attn_body(q_ref, k_ref, v_ref, o_ref, kt_scr, s_scr, p_scr, m_scr, acc_scr, *, tk, rb):
    tq = q_ref.shape[0]
    nk = k_ref.shape[0] // tk
    nc = tk // LANES
    m_scr[...] = jnp.full(m_scr.shape, -jnp.inf, F32)
    acc_scr[...] = jnp.zeros(acc_scr.shape, F32)

    @pl.when(pl.program_id(2) == 0)
    def _():
        for c in range(nk):
            rows = slice(c * tk, (c + 1) * tk)
            kt_scr[:, rows] = k_ref[rows, :].astype(F32).T.astype(BF16)

    def step(j, carry):
        off = pl.multiple_of(j * tk, tk)
        for h in range(2):
            hs = slice(h * HEAD_PAD, (h + 1) * HEAD_PAD)
            s_scr[h] = jnp.dot(q_ref[:, hs], kt_scr[hs, pl.ds(off, tk)], preferred_element_type=F32)
        for h in range(2):
            hs = slice(h * HEAD_PAD, (h + 1) * HEAD_PAD)
            for r in range(tq // rb):
                rows = slice(r * rb, (r + 1) * rb)
                mx = s_scr[h, rows, 0:LANES]
                for c in range(1, nc):
                    mx = jnp.maximum(mx, s_scr[h, rows, c * LANES:(c + 1) * LANES])
                m_old = m_scr[h, rows, :]
                m_new = jnp.maximum(m_old, jnp.max(mx, axis=-1, keepdims=True))
                m_scr[h, rows, :] = m_new
                acc_scr[h, rows, :] = acc_scr[h, rows, :] * jnp.exp2(m_old - m_new)
                for c in range(nc):
                    cs = slice(c * LANES, (c + 1) * LANES)
                    p_scr[h, rows, cs] = jnp.exp2(s_scr[h, rows, cs] - m_new).astype(BF16)
            acc_scr[h] += jnp.dot(p_scr[h], v_ref[pl.ds(off, tk), hs], preferred_element_type=F32)
        return carry

    lax.fori_loop(0, nk, step, 0, unroll=2)
    a0, a1 = acc_scr[0], acc_scr[1]
    lane = lax.broadcasted_iota(jnp.int32, a0.shape, 1)
    o_ref[...] = jnp.where(lane < V_DIM, a0 / a0[:, V_DIM:V_DIM + 1], a1 / a1[:, 0:1])


def _attn(q, k, v, tq, tk, rb):
    B, L, _ = q.shape
    pair = 2 * HEAD_PAD
    return pl.pallas_call(
        functools.partial(_attn_body, tk=tk, rb=rb),
        grid=(B, N_HEADS // 2, L // tq),
        in_specs=[pl.BlockSpec((None, tq, pair), lambda b, h, i: (b, i, h)),
                  pl.BlockSpec((None, L, pair), lambda b, h, i: (b, 0, h)),
                  pl.BlockSpec((None, L, pair), lambda b, h, i: (b, 0, h))],
        out_specs=pl.BlockSpec((None, tq, 2 * V_DIM), lambda b, h, i: (b, i, h)),
        out_shape=jax.ShapeDtypeStruct((B, L, N_HEADS * V_DIM), F32),
        scratch_shapes=[pltpu.VMEM((pair, L), BF16), pltpu.VMEM((2, tq, tk), F32), pltpu.VMEM((2, tq, tk), BF16),
                        pltpu.VMEM((2, tq, HEAD_PAD), F32), pltpu.VMEM((2, tq, HEAD_PAD), F32)],
        compiler_params=_params("parallel", "parallel", "arbitrary"),
        name="attn",
    )(q, k, v)


def _tail_body(x_ref, yh_ref, om_ref, p_ref, hnw_ref, mnw_ref, wo_ref, mlw_ref, w1_ref, w2_ref, wg_ref,
               wp_ref, pnw_ref, o_ref, *, fc):
    dot = functools.partial(jnp.dot, preferred_element_type=F32)
    a = _rms(yh_ref[...], hnw_ref[...]).astype(BF16)
    b = _rms(om_ref[...], mnw_ref[...]).astype(BF16)
    x = x_ref[...] + dot(a, wo_ref[:HY_WIDTH, :]) + dot(b, wo_ref[HY_WIDTH:, :])
    xn = _rms(x, mlw_ref[...]).astype(BF16)
    mlp = jnp.zeros_like(x)
    for c in range(w1_ref.shape[1] // fc):
        h = jnp.maximum(dot(xn, w1_ref[:, c * fc:(c + 1) * fc]), 0.0)
        mlp = mlp + dot((h * h).astype(BF16), w2_ref[c * fc:(c + 1) * fc, :])
    x = x + mlp
    gate = jax.nn.sigmoid(dot(x.astype(BF16), wg_ref[...]))
    pp = _rms(dot(p_ref[...].astype(BF16), wp_ref[...]), pnw_ref[...])
    o_ref[...] = x + gate * pp


def _tail(x, yh, om, p, hnw, mnw, wo, mlw, w1, w2, wg, wp, pnw, tm, fc):
    B, L, D = x.shape
    tok = lambda w: pl.BlockSpec((None, tm, w), lambda b, t: (b, t, 0))
    res = lambda a: pl.BlockSpec(a.shape, lambda b, t: (0,) * a.ndim, pipeline_mode=pl.Buffered(1))
    return pl.pallas_call(
        functools.partial(_tail_body, fc=fc),
        grid=(B, L // tm),
        in_specs=[tok(D), tok(yh.shape[-1]), tok(om.shape[-1]), tok(p.shape[-1]),
                  res(hnw), res(mnw), res(wo), res(mlw), res(w1), res(w2), res(wg), res(wp), res(pnw)],
        out_specs=tok(D),
        out_shape=jax.ShapeDtypeStruct((B, L, D), F32),
        compiler_params=_params("parallel", "parallel"),
        name="tail",
    )(x, yh, om, p, hnw, mnw, wo, mlw, w1, w2, wg, wp, pnw)


def _rope_tables(L):
    inv = ROPE_THETA ** (-(2.0 * jnp.arange(HALF_ROPE, dtype=F32)) / QK_ROPE)
    ang = jnp.arange(L, dtype=F32)[:, None] * inv[None, :]
    cos, sin = jnp.cos(ang), jnp.sin(ang)
    one = jnp.ones((L, QK_NOPE), F32)
    pad = jnp.zeros((L, HEAD_PAD - QK_DIM), F32)
    cosf = jnp.concatenate([one, cos, cos, pad], axis=1)
    sinf = jnp.concatenate([0.0 * one, -sin, sin, pad], axis=1)
    return cosf, sinf


def _pair_weights(w):
    z = jnp.zeros_like(w)
    return jnp.block([[w, z], [z, w]])


def _filter_features(L):
    pos = jnp.concatenate([jnp.arange(L), jnp.arange(L - 1, -1, -1)]).astype(F32)[:, None]
    t = pos / (L - 1)
    w = (2.0 * math.pi) * pos / L
    f = jnp.linspace(1e-4, FILT_BANDS - 1, FILT_BANDS, dtype=F32)[None, :]
    z = jnp.concatenate([t, jnp.cos(f * w), -jnp.sin(f * w)], axis=-1)
    return jnp.pad(z, ((0, 0), (0, LANES - z.shape[1])))


def _decay_rates():
    min_decay = math.log(DECAY_TARGET) / FAST_DECAY_PCT
    max_decay = math.log(DECAY_TARGET) / SLOW_DECAY_PCT
    return jnp.abs(jnp.linspace(min_decay, max_decay, HY_WIDTH, dtype=F32))[None, :]


def _pad_heads(w, width, lead=0):
    w = w.reshape(w.shape[:-1] + (N_HEADS, width))
    pad = [(0, 0)] * (w.ndim - 1) + [(lead, HEAD_PAD - width - lead)]
    return jnp.pad(w, pad).reshape(w.shape[:-2] + (N_HEADS * HEAD_PAD,))


def _rope_partner(w):
    a, b = w[..., QK_NOPE:QK_NOPE + HALF_ROPE], w[..., QK_NOPE + HALF_ROPE:]
    zeros = lambda n: jnp.zeros(w.shape[:-1] + (n,), w.dtype)
    return jnp.concatenate([zeros(QK_NOPE), b, a, zeros(HEAD_PAD - QK_DIM)], axis=-1)


def _layer_weights(attn_norm_w, w_in, q_a_norm_w, w_q_b, kv_a_norm_w, w_kv_b, q_norm_w, k_norm_w):
    d = w_in.shape[0]
    o3 = HY3 + Q_LORA + KV_LORA
    rope_cols = jnp.pad(w_in[:, o3:], ((0, 0), (QK_NOPE, 0)))
    win = jnp.concatenate([w_in[:, :o3], jnp.pad(rope_cols, ((0, 0), (0, HEAD_PAD - QK_DIM))),
                           _rope_partner(rope_cols)], axis=1).astype(BF16)
    wq = w_q_b.reshape(Q_LORA, N_HEADS, QK_DIM)
    wqb = jnp.concatenate([_pad_heads(w_q_b, QK_DIM), _rope_partner(wq).reshape(Q_LORA, N_HEADS * HEAD_PAD)],
                          axis=1).astype(BF16)
    kv = w_kv_b.reshape(KV_LORA, N_HEADS, QK_NOPE + V_DIM)
    wk = _pad_heads(kv[:, :, :QK_NOPE].reshape(KV_LORA, N_HEADS * QK_NOPE), QK_NOPE)
    v4 = kv[:, :, QK_NOPE:].reshape(KV_LORA, N_HEADS // 2, 2, V_DIM)
    zero = jnp.zeros_like(v4[:, :, 0])
    wv = jnp.stack([v4[:, :, 0], zero, zero, v4[:, :, 1]], axis=2).reshape(KV_LORA, N_HEADS * HEAD_PAD)
    wkvb = jnp.concatenate([wk, wv], axis=1).astype(BF16)
    norm2 = lambda w: jnp.stack([jnp.pad(w, (0, HEAD_PAD - QK_DIM)), _rope_partner(w)])
    return (attn_norm_w[None, :], win, q_a_norm_w[None, :], wqb, kv_a_norm_w[None, :], wkvb,
            norm2(q_norm_w), norm2(k_norm_w))


def _group(x, p, lw, conv_w, conv_b, filt, filt_bias, tailw):
    B, L, D = x.shape
    cosf, sinf = _rope_tables(L)
    x0, z, q, k, v = _inproj(x, lw[0], lw[1], conv_w, conv_b, *lw[2:], cosf, sinf, tm=512)
    kern = _filt(_filter_features(L), *filt, _decay_rates(), tf=512)
    yh = _hyena_conv(x0, z, kern, filt_bias, kb=8, rb=16, cw=HY_WIDTH)
    om = _attn(q, k, v, tq=512, tk=min(1024, L), rb=128)
    return _tail(x, yh, om, p, *tailw, tm=512, fc=1024)


def kernel(x_prompt, x_sample, p_prompt, p_sample, attn_norm_w, w_in, conv_w, conv_b, filt_w1, filt_b1, filt_freq, filt_w2, filt_b2, filt_w3, filt_bias, q_a_norm_w, w_q_b, kv_a_norm_w, w_kv_b, q_norm_w, k_norm_w, hy_out_norm_w, mla_out_norm_w, w_out, mlp_norm_w, w_mlp1, w_mlp2, w_ple_gate, w_ple_proj, ple_norm_w):
    depth = w_in.shape[0]
    for i in range(depth):
        lw = _layer_weights(attn_norm_w[i], w_in[i], q_a_norm_w[i], w_q_b[i], kv_a_norm_w[i], w_kv_b[i],
                            q_norm_w[i], k_norm_w[i])
        w1 = jnp.pad(filt_w1[i], ((0, LANES - filt_w1.shape[1]), (0, 0)))
        twice = lambda v: jnp.concatenate([v, v])[None, :]
        w3 = jnp.stack([_pair_weights(filt_w3[i][:, :HY_WIDTH]), _pair_weights(filt_w3[i][:, HY_WIDTH:])])
        filt = (_pair_weights(w1), twice(filt_b1[i]), twice(filt_freq[i]), _pair_weights(filt_w2[i]),
                twice(filt_b2[i]), w3)
        tailw = (hy_out_norm_w[i][None, :], mla_out_norm_w[i][None, :], w_out[i].astype(BF16),
                 mlp_norm_w[i][None, :], w_mlp1[i].astype(BF16), w_mlp2[i].astype(BF16),
                 w_ple_gate[i].astype(BF16), w_ple_proj[i].astype(BF16), ple_norm_w[i][None, :])
        x_prompt = _group(x_prompt, p_prompt[i], lw, conv_w[i], conv_b[i][None, :], filt, filt_bias[i], tailw)
        x_sample = _group(x_sample, p_sample[i], lw, conv_w[i], conv_b[i][None, :], filt, filt_bias[i], tailw)
    return (x_prompt, x_sample)
```

```python
import functools
import math

import jax
import jax.numpy as jnp
from jax import lax
from jax.experimental import pallas as pl
from jax.experimental.pallas import tpu as pltpu

F32 = jnp.float32
BF16 = jnp.bfloat16
HIGHEST = lax.Precision.HIGHEST

LANES = 128
SUBLANES = 8
VMEM_LIMIT = 56 * 1024 * 1024

HY_WIDTH = 512
HY_STREAMS = 3
HY3 = HY_STREAMS * HY_WIDTH
FILT_BANDS = 16
FILT_HIDDEN = 64
DECAY_TARGET = 1e-2
FAST_DECAY_PCT = 0.3
SLOW_DECAY_PCT = 1.5
N_HEADS = 8
QK_NOPE = 64
QK_ROPE = 32
QK_DIM = QK_NOPE + QK_ROPE
V_DIM = 64
Q_LORA = 256
KV_LORA = 128
ROPE_THETA = 10000.0
EPS = 1e-6
HEAD_PAD = LANES
HALF_ROPE = QK_ROPE // 2
INPROJ_SUBTILES = 4


def _rms(x, w):
    ms = jnp.mean(x * x, axis=-1, keepdims=True)
    return x * lax.rsqrt(ms + EPS) * w


def _params(*sem):
    return pltpu.CompilerParams(dimension_semantics=sem, vmem_limit_bytes=VMEM_LIMIT)


def _const_spec(shape):
    nd = len(shape)
    return pl.BlockSpec(shape, lambda *_: (0,) * nd)


def _inproj_body(xp_ref, x_ref, xn_ref, anw_ref, win_ref, cw_ref, cb_ref, qaw_ref, wqb_ref, kvaw_ref, wkvb_ref,
                 qnw_ref, knw_ref, cos_ref, sin_ref, x0_ref, z_ref, q_ref, k_ref, v_ref):
    t = pl.program_id(1)
    tm = x_ref.shape[0]
    has_prev = (t > 0).astype(F32)
    has_next = (t < pl.num_programs(1) - 1).astype(F32)
    xa = jnp.concatenate([xp_ref[...] * has_prev, x_ref[...], xn_ref[...] * has_next], axis=0)
    kw = N_HEADS * HEAD_PAD
    lane = lax.broadcasted_iota(jnp.int32, (1, kw), 1) % (2 * HEAD_PAD)
    ones_col = (lane == V_DIM) | (lane == HEAD_PAD)
    qscale = QK_DIM ** -0.5 * math.log2(math.e)
    o2 = HY3 + Q_LORA
    o3 = o2 + KV_LORA
    o4 = o3 + HEAD_PAD

    def rstd(xh):
        return lax.rsqrt(jnp.sum(xh * xh, axis=-1, keepdims=True) / QK_DIM + EPS)

    def sub_tile(out, sub):
        rows = sub + 2 * SUBLANES
        proj = jnp.dot(_rms(xa[out:out + rows], anw_ref[...]).astype(BF16), win_ref[...],
                       preferred_element_type=F32)
        body = slice(SUBLANES, SUBLANES + sub)
        dst = slice(out, out + sub)

        def stream(s):
            sl = slice(s * HY_WIDTH, (s + 1) * HY_WIDTH)
            h = proj[:, sl]
            dn, up = pltpu.roll(h, 1, 0)[body], pltpu.roll(h, rows - 1, 0)[body]
            return cb_ref[:, sl] + dn * cw_ref[0:1, sl] + h[body] * cw_ref[1:2, sl] + up * cw_ref[2:3, sl]

        x0_ref[dst] = stream(0)
        z_ref[dst] = stream(2) * stream(1)

        q_lat, kv_lat, krope, krope_pi = proj[body, HY3:o2], proj[body, o2:o3], proj[body, o3:o4], proj[body, o4:]
        qf = jnp.dot(_rms(q_lat, qaw_ref[...]).astype(BF16), wqb_ref[...], preferred_element_type=F32)
        kvf = jnp.dot(_rms(kv_lat, kvaw_ref[...]).astype(BF16), wkvb_ref[...], preferred_element_type=F32)
        v_ref[dst] = jnp.where(ones_col, 1.0, kvf[:, kw:]).astype(BF16)

        cosf, sinf = cos_ref[dst], sin_ref[dst]
        cq, sq = cosf * (qnw_ref[0:1] * qscale), sinf * (qnw_ref[1:2] * qscale)
        ck, sk = cosf * knw_ref[0:1], sinf * knw_ref[1:2]
        k_pi = krope_pi * sk
        for h in range(N_HEADS):
            sl = slice(h * HEAD_PAD, (h + 1) * HEAD_PAD)
            xq = qf[:, sl]
            q_pi = qf[:, kw + h * HEAD_PAD:kw + (h + 1) * HEAD_PAD]
            q_ref[dst, sl] = ((xq * cq + q_pi * sq) * rstd(xq)).astype(BF16)
            xk = kvf[:, sl] + krope
            k_ref[dst, sl] = ((xk * ck + k_pi) * rstd(xk)).astype(BF16)

    sub = tm // INPROJ_SUBTILES
    for j in range(INPROJ_SUBTILES):
        sub_tile(j * sub, sub)


def _inproj(x, anw, win, conv_w, conv_b, qaw, wqb, kvaw, wkvb, qnw, knw, cosf, sinf, tm):
    B, L, D = x.shape
    kw = N_HEADS * HEAD_PAD
    nb = tm // SUBLANES
    last = L // SUBLANES - 1
    tok = lambda w: pl.BlockSpec((None, tm, w), lambda b, t: (b, t, 0))
    pos = pl.BlockSpec((tm, HEAD_PAD), lambda b, t: (t, 0))
    consts = [anw, win, conv_w, conv_b, qaw, wqb, kvaw, wkvb, qnw, knw]
    return pl.pallas_call(
        _inproj_body,
        grid=(B, L // tm),
        in_specs=[pl.BlockSpec((None, SUBLANES, D), lambda b, t: (b, jnp.maximum(t * nb - 1, 0), 0)), tok(D),
                  pl.BlockSpec((None, SUBLANES, D), lambda b, t: (b, jnp.minimum((t + 1) * nb, last), 0))]
                 + [_const_spec(c.shape) for c in consts] + [pos, pos],
        out_specs=[tok(HY_WIDTH), tok(HY_WIDTH), tok(kw), tok(kw), tok(kw)],
        out_shape=[jax.ShapeDtypeStruct((B, L, HY_WIDTH), F32)] * 2 + [jax.ShapeDtypeStruct((B, L, kw), BF16)] * 3,
        compiler_params=_params("parallel", "parallel"),
        name="inproj",
    )(x, x, x, *consts, cosf, sinf)


def _filt_body(z_ref, w1_ref, b1_ref, fr_ref, w2_ref, b2_ref, w3_ref, dl_ref, o_ref):
    hr = z_ref.shape[0] // 2
    top, bot = z_ref[:hr], z_ref[hr:]
    fr = fr_ref[...]
    dot = functools.partial(jnp.dot, precision=HIGHEST, preferred_element_type=F32)
    h = jnp.sin(fr * (dot(jnp.concatenate([top, bot], axis=1), w1_ref[...]) + b1_ref[...]))
    h = jnp.sin(fr * (dot(h, w2_ref[...]) + b2_ref[...]))
    h = dot(h, w3_ref[...])
    dl = dl_ref[...]
    o_ref[:hr] = h[:, :HY_WIDTH] * jnp.exp(-top[:, 0:1] * dl)
    o_ref[hr:] = h[:, HY_WIDTH:] * jnp.exp(-bot[:, 0:1] * dl)


def _filt(feat, w1, b1, fr, w2, b2, w3, deltas, tf):
    n2l = feat.shape[0]
    half = n2l // tf // 2
    return pl.pallas_call(
        _filt_body,
        grid=(n2l // tf,),
        in_specs=[pl.BlockSpec((tf, LANES), lambda t: (t, 0)), _const_spec(w1.shape), _const_spec(b1.shape),
                  _const_spec(fr.shape), _const_spec(w2.shape), _const_spec(b2.shape),
                  pl.BlockSpec((None,) + w3.shape[1:], lambda t: (t // half, 0, 0)),
                  _const_spec(deltas.shape)],
        out_specs=pl.BlockSpec((tf, HY_WIDTH), lambda t: (t, 0)),
        out_shape=jax.ShapeDtypeStruct((n2l, HY_WIDTH), F32),
        compiler_params=_params("parallel"),
        name="filt",
    )(feat, w1, b1, fr, w2, b2, w3, deltas)


def _hi_lo(x):
    hi = x.astype(BF16)
    return hi, (x - hi.astype(F32)).astype(BF16)


def _dot3(m_ref, x):
    xh, xl = _hi_lo(x)
    mh = m_ref[0]
    dot = functools.partial(jnp.dot, preferred_element_type=F32)
    return dot(mh, xh) + dot(mh, xl) + dot(m_ref[1], xh)


def _dft1_body(c_ref, x_ref, o_ref):
    k, rb, cw = x_ref.shape
    m = o_ref.shape[0]
    x2 = x_ref.reshape(k * rb, cw)
    o2 = o_ref.reshape(m * rb, cw)
    for s in range(rb):
        o2[pl.ds(s, m, stride=rb), :] = _dot3(c_ref, x2[pl.ds(s, k, stride=rb), :])


def _dft1(cmat, xv, rb, cw):
    P, K, n2, C = xv.shape
    M = cmat.shape[1]
    return pl.pallas_call(
        _dft1_body,
        grid=(P, n2 // rb, C // cw),
        in_specs=[_const_spec(cmat.shape), pl.BlockSpec((None, K, rb, cw), lambda p, i, c: (p, 0, i, c))],
        out_specs=pl.BlockSpec((None, M, rb, cw), lambda p, i, c: (p, 0, i, c)),
        out_shape=jax.ShapeDtypeStruct((P, M, n2, C), F32),
        compiler_params=_params("parallel", "parallel", "parallel"),
        name="dft1",
    )(cmat, xv)


def _cmul(ar, ai, br, bi):
    return ar * br - ai * bi, ar * bi + ai * br


def _fwd2(f_ref, tw_ref, a_ref, k):
    n2, cw = a_ref.shape[2:]
    reps = cw // LANES
    twr = jnp.concatenate([tw_ref[0, k]] * reps, axis=1)
    twi = jnp.concatenate([tw_ref[1, k]] * reps, axis=1)
    ar, ai = _cmul(a_ref[0, k], a_ref[1, k], twr, twi)
    x = _dot3(f_ref, jnp.concatenate([ar, ai], axis=0))
    return x[:n2], x[n2:], twr, twi


def _mid_kernel_body(f_ref, tw_ref, a_ref, o_ref):
    for k in range(a_ref.shape[1]):
        o_ref[0, k], o_ref[1, k], _, _ = _fwd2(f_ref, tw_ref, a_ref, k)


def _mid_body(f_ref, ft_ref, tw_ref, kf_ref, a_ref, o_ref):
    n2 = a_ref.shape[2]
    for k in range(a_ref.shape[1]):
        xr, xi, twr, twi = _fwd2(f_ref, tw_ref, a_ref, k)
        yr, yi = _cmul(xr, xi, kf_ref[0, k], kf_ref[1, k])
        b = _dot3(ft_ref, jnp.concatenate([yr, yi], axis=0))
        o_ref[0, k], o_ref[1, k] = _cmul(b[:n2], b[n2:], twr, -twi)


def _mid(f2, f2_t, tw, kf, a, kb, cw):
    P, _, n1, n2, C = a.shape
    dat = pl.BlockSpec((None, 2, kb, n2, cw), lambda k, c, p: (p, 0, k, 0, c))
    tws = pl.BlockSpec((2, kb, n2, LANES), lambda k, c, p: (0, k, 0, 0))
    mat = _const_spec(f2.shape)
    if kf is None:
        body, ins, specs = _mid_kernel_body, (f2, tw, a), [mat, tws, dat]
    else:
        kspec = pl.BlockSpec((None, 2, kb, n2, cw), lambda k, c, p: (0, 0, k, 0, c))
        body, ins, specs = _mid_body, (f2, f2_t, tw, kf, a), [mat, mat, tws, kspec, dat]
    return pl.pallas_call(
        body,
        grid=(n1 // kb, C // cw, P),
        in_specs=specs,
        out_specs=dat,
        out_shape=jax.ShapeDtypeStruct(a.shape, F32),
        compiler_params=_params("parallel", "parallel", "parallel"),
        name="mid",
    )(*ins)


def _dft_tables(n1, n2):
    n = n1 * n2
    h = n1 // 2
    i1 = jnp.arange(n1, dtype=jnp.int32)
    th = (2.0 * math.pi / n1) * ((i1[:, None] * i1[None, :]) % n1).astype(F32)
    c1, s1 = jnp.cos(th), jnp.sin(th)
    cz = jnp.block([[c1[:, :h], s1[:, :h]], [-s1[:, :h], c1[:, :h]]])
    ck = jnp.concatenate([c1, -s1], axis=0)
    c4 = jnp.block([[c1[:h], -s1[:h]], [s1[:h], c1[:h]]]) * (1.0 / n)
    u = jnp.arange(n2, dtype=jnp.int32)
    ph2 = (2.0 * math.pi / n2) * ((u[:, None] * u[None, :]) % n2).astype(F32)
    c2, s2 = jnp.cos(ph2), jnp.sin(ph2)
    f2 = jnp.block([[c2, s2], [-s2, c2]])
    pht = (2.0 * math.pi / n) * (i1[:, None] * u[None, :]).astype(F32)
    tw = jnp.stack([jnp.cos(pht), -jnp.sin(pht)])
    tw = jnp.broadcast_to(tw[..., None], tw.shape + (LANES,))
    parts = lambda m: jnp.stack(_hi_lo(m))
    return parts(cz), parts(ck), parts(c4), parts(f2), parts(f2.T), tw


def _split(L):
    n = 2 * L
    n1 = 1 << ((n.bit_length() - 1) // 2)
    return n1, n // n1


def _hyena_conv(z, kern, kb, rb, cw):
    B, L, C = z.shape
    n1, n2 = _split(L)
    P = B // 2
    cz, ck, c4, f2, f2_t, tw = _dft_tables(n1, n2)
    ka = _dft1(ck, kern.reshape(1, n1, n2, C), rb, LANES)
    kf = _mid(f2, None, tw, None, ka.reshape(1, 2, n1, n2, C), kb, cw)
    za = _dft1(cz, z.reshape(P, n1, n2, C), rb, LANES)
    zb = _mid(f2, f2_t, tw, kf, za.reshape(P, 2, n1, n2, C), kb, cw)
    return _dft1(c4, zb.reshape(P, 2 * n1, n2, C), rb, LANES).reshape(B, L, C)


def _attn_body(q_ref, k_ref, v_ref, o_ref, s_scr, p_scr, m_scr, acc_scr, *, tk, rb):
    tq = q_ref.shape[0]
    nk = k_ref.shape[0] // tk
    nc = tk // LANES
    m_scr[...] = jnp.full(m_scr.shape, -jnp.inf, F32)
    acc_scr[...] = jnp.zeros(acc_scr.shape, F32)

    def step(j, carry):
        off = pl.multiple_of(j * tk, tk)
        for h in range(2):
            hs = slice(h * HEAD_PAD, (h + 1) * HEAD_PAD)
            s_scr[h] = lax.dot_general(q_ref[:, hs], k_ref[pl.ds(off, tk), hs], (((1,), (1,)), ((), ())),
                                       preferred_element_type=F32)
        for h in range(2):
            hs = slice(h * HEAD_PAD, (h + 1) * HEAD_PAD)
            for r in range(tq // rb):
                rows = slice(r * rb, (r + 1) * rb)
                mx = s_scr[h, rows, 0:LANES]
                for c in range(1, nc):
                    mx = jnp.maximum(mx, s_scr[h, rows, c * LANES:(c + 1) * LANES])
                m_old = m_scr[h, rows, :]
                m_new = jnp.maximum(m_old, jnp.max(mx, axis=-1, keepdims=True))
                m_scr[h, rows, :] = m_new
                acc_scr[h, rows, :] = acc_scr[h, rows, :] * jnp.exp2(m_old - m_new)
                for c in range(nc):
                    cs = slice(c * LANES, (c + 1) * LANES)
                    p_scr[h, rows, cs] = jnp.exp2(s_scr[h, rows, cs] - m_new).astype(BF16)
            acc_scr[h] += jnp.dot(p_scr[h], v_ref[pl.ds(off, tk), hs], preferred_element_type=F32)
        return carry

    lax.fori_loop(0, nk, step, 0, unroll=2)
    a0, a1 = acc_scr[0], acc_scr[1]
    lane = lax.broadcasted_iota(jnp.int32, a0.shape, 1)
    o_ref[...] = jnp.where(lane < V_DIM, a0 / a0[:, V_DIM:V_DIM + 1], a1 / a1[:, 0:1])


def _attn(q, k, v, tq, tk, rb):
    B, L, _ = q.shape
    pair = 2 * HEAD_PAD
    return pl.pallas_call(
        functools.partial(_attn_body, tk=tk, rb=rb),
        grid=(B, N_HEADS // 2, L // tq),
        in_specs=[pl.BlockSpec((None, tq, pair), lambda b, h, i: (b, i, h)),
                  pl.BlockSpec((None, L, pair), lambda b, h, i: (b, 0, h)),
                  pl.BlockSpec((None, L, pair), lambda b, h, i: (b, 0, h))],
        out_specs=pl.BlockSpec((None, tq, 2 * V_DIM), lambda b, h, i: (b, i, h)),
        out_shape=jax.ShapeDtypeStruct((B, L, N_HEADS * V_DIM), F32),
        scratch_shapes=[pltpu.VMEM((2, tq, tk), F32), pltpu.VMEM((2, tq, tk), BF16),
                        pltpu.VMEM((2, tq, HEAD_PAD), F32), pltpu.VMEM((2, tq, HEAD_PAD), F32)],
        compiler_params=_params("parallel", "parallel", "parallel"),
        name="attn",
    )(q, k, v)


def _tail_body(x_ref, yc_ref, x0_ref, z_ref, om_ref, p_ref, fb_ref, hnw_ref, mnw_ref, wo_ref, mlw_ref, w1_ref,
               w2_ref, wg_ref, wp_ref, pnw_ref, o_ref, *, fc):
    dot = functools.partial(jnp.dot, preferred_element_type=F32)
    yh = x0_ref[...] * (yc_ref[...] + z_ref[...] * fb_ref[...])
    a = _rms(yh, hnw_ref[...]).astype(BF16)
    b = _rms(om_ref[...], mnw_ref[...]).astype(BF16)
    x = x_ref[...] + dot(a, wo_ref[:HY_WIDTH, :]) + dot(b, wo_ref[HY_WIDTH:, :])
    xn = _rms(x, mlw_ref[...]).astype(BF16)
    mlp = jnp.zeros_like(x)
    for c in range(w1_ref.shape[1] // fc):
        h = jnp.maximum(dot(xn, w1_ref[:, c * fc:(c + 1) * fc]), 0.0)
        mlp = mlp + dot((h * h).astype(BF16), w2_ref[c * fc:(c + 1) * fc, :])
    x = x + mlp
    gate = jax.nn.sigmoid(dot(x.astype(BF16), wg_ref[...]))
    pp = _rms(dot(p_ref[...].astype(BF16), wp_ref[...]), pnw_ref[...])
    o_ref[...] = x + gate * pp


def _tail(x, yc, x0, z, om, p, fb, hnw, mnw, wo, mlw, w1, w2, wg, wp, pnw, tm, fc):
    B, L, D = x.shape
    tok = lambda a: pl.BlockSpec((None, tm, a.shape[-1]), lambda b, t: (b, t, 0))
    res = lambda a: pl.BlockSpec(a.shape, lambda b, t: (0,) * a.ndim, pipeline_mode=pl.Buffered(1))
    toks = (x, yc, x0, z, om, p)
    consts = (fb, hnw, mnw, wo, mlw, w1, w2, wg, wp, pnw)
    return pl.pallas_call(
        functools.partial(_tail_body, fc=fc),
        grid=(B, L // tm),
        in_specs=[tok(a) for a in toks] + [res(a) for a in consts],
        out_specs=tok(x),
        out_shape=jax.ShapeDtypeStruct((B, L, D), F32),
        compiler_params=_params("parallel", "parallel"),
        name="tail",
    )(*toks, *consts)


def _rope_tables(L):
    inv = ROPE_THETA ** (-(2.0 * jnp.arange(HALF_ROPE, dtype=F32)) / QK_ROPE)
    ang = jnp.arange(L, dtype=F32)[:, None] * inv[None, :]
    cos, sin = jnp.cos(ang), jnp.sin(ang)
    one = jnp.ones((L, QK_NOPE), F32)
    pad = jnp.zeros((L, HEAD_PAD - QK_DIM), F32)
    cosf = jnp.concatenate([one, cos, cos, pad], axis=1)
    sinf = jnp.concatenate([0.0 * one, -sin, sin, pad], axis=1)
    return cosf, sinf


def _pair_weights(w):
    z = jnp.zeros_like(w)
    return jnp.block([[w, z], [z, w]])


def _filter_features(L):
    pos = jnp.concatenate([jnp.arange(L), jnp.arange(L - 1, -1, -1)]).astype(F32)[:, None]
    t = pos / (L - 1)
    w = (2.0 * math.pi) * pos / L
    f = jnp.linspace(1e-4, FILT_BANDS - 1, FILT_BANDS, dtype=F32)[None, :]
    z = jnp.concatenate([t, jnp.cos(f * w), -jnp.sin(f * w)], axis=-1)
    return jnp.pad(z, ((0, 0), (0, LANES - z.shape[1])))


def _decay_rates():
    min_decay = math.log(DECAY_TARGET) / FAST_DECAY_PCT
    max_decay = math.log(DECAY_TARGET) / SLOW_DECAY_PCT
    return jnp.abs(jnp.linspace(min_decay, max_decay, HY_WIDTH, dtype=F32))[None, :]


def _pad_heads(w, width, lead=0):
    w = w.reshape(w.shape[:-1] + (N_HEADS, width))
    pad = [(0, 0)] * (w.ndim - 1) + [(lead, HEAD_PAD - width - lead)]
    return jnp.pad(w, pad).reshape(w.shape[:-2] + (N_HEADS * HEAD_PAD,))


def _rope_partner(w):
    a, b = w[..., QK_NOPE:QK_NOPE + HALF_ROPE], w[..., QK_NOPE + HALF_ROPE:]
    zeros = lambda n: jnp.zeros(w.shape[:-1] + (n,), w.dtype)
    return jnp.concatenate([zeros(QK_NOPE), b, a, zeros(HEAD_PAD - QK_DIM)], axis=-1)


def _layer_weights(attn_norm_w, w_in, q_a_norm_w, w_q_b, kv_a_norm_w, w_kv_b, q_norm_w, k_norm_w):
    d = w_in.shape[0]
    o3 = HY3 + Q_LORA + KV_LORA
    rope_cols = jnp.pad(w_in[:, o3:], ((0, 0), (QK_NOPE, 0)))
    win = jnp.concatenate([w_in[:, :o3], jnp.pad(rope_cols, ((0, 0), (0, HEAD_PAD - QK_DIM))),
                           _rope_partner(rope_cols)], axis=1).astype(BF16)
    wq = w_q_b.reshape(Q_LORA, N_HEADS, QK_DIM)
    wqb = jnp.concatenate([_pad_heads(w_q_b, QK_DIM), _rope_partner(wq).reshape(Q_LORA, N_HEADS * HEAD_PAD)],
                          axis=1).astype(BF16)
    kv = w_kv_b.reshape(KV_LORA, N_HEADS, QK_NOPE + V_DIM)
    wk = _pad_heads(kv[:, :, :QK_NOPE].reshape(KV_LORA, N_HEADS * QK_NOPE), QK_NOPE)
    v4 = kv[:, :, QK_NOPE:].reshape(KV_LORA, N_HEADS // 2, 2, V_DIM)
    zero = jnp.zeros_like(v4[:, :, 0])
    wv = jnp.stack([v4[:, :, 0], zero, zero, v4[:, :, 1]], axis=2).reshape(KV_LORA, N_HEADS * HEAD_PAD)
    wkvb = jnp.concatenate([wk, wv], axis=1).astype(BF16)
    norm2 = lambda w: jnp.stack([jnp.pad(w, (0, HEAD_PAD - QK_DIM)), _rope_partner(w)])
    return (attn_norm_w[None, :], win, q_a_norm_w[None, :], wqb, kv_a_norm_w[None, :], wkvb,
            norm2(q_norm_w), norm2(k_norm_w))


def _group(x, p, lw, conv_w, conv_b, filt, filt_bias, tailw):
    B, L, D = x.shape
    cosf, sinf = _rope_tables(L)
    x0, z, q, k, v = _inproj(x, lw[0], lw[1], conv_w, conv_b, *lw[2:], cosf, sinf, tm=1024)
    kern = _filt(_filter_features(L), *filt, _decay_rates(), tf=512)
    yc = _hyena_conv(z, kern, kb=8, rb=16, cw=HY_WIDTH)
    om = _attn(q, k, v, tq=512, tk=min(1024, L), rb=128)
    return _tail(x, yc, x0, z, om, p, filt_bias[None, :], *tailw, tm=512, fc=1024)


def kernel(x_prompt, x_sample, p_prompt, p_sample, attn_norm_w, w_in, conv_w, conv_b, filt_w1, filt_b1, filt_freq, filt_w2, filt_b2, filt_w3, filt_bias, q_a_norm_w, w_q_b, kv_a_norm_w, w_kv_b, q_norm_w, k_norm_w, hy_out_norm_w, mla_out_norm_w, w_out, mlp_norm_w, w_mlp1, w_mlp2, w_ple_gate, w_ple_proj, ple_norm_w):
    depth = w_in.shape[0]
    for i in range(depth):
        lw = _layer_weights(attn_norm_w[i], w_in[i], q_a_norm_w[i], w_q_b[i], kv_a_norm_w[i], w_kv_b[i],
                            q_norm_w[i], k_norm_w[i])
        w1 = jnp.pad(filt_w1[i], ((0, LANES - filt_w1.shape[1]), (0, 0)))
        twice = lambda v: jnp.concatenate([v, v])[None, :]
        w3 = jnp.stack([_pair_weights(filt_w3[i][:, :HY_WIDTH]), _pair_weights(filt_w3[i][:, HY_WIDTH:])])
        filt = (_pair_weights(w1), twice(filt_b1[i]), twice(filt_freq[i]), _pair_weights(filt_w2[i]),
                twice(filt_b2[i]), w3)
        tailw = (hy_out_norm_w[i][None, :], mla_out_norm_w[i][None, :], w_out[i].astype(BF16),
                 mlp_norm_w[i][None, :], w_mlp1[i].astype(BF16), w_mlp2[i].astype(BF16),
                 w_ple_gate[i].astype(BF16), w_ple_proj[i].astype(BF16), ple_norm_w[i][None, :])
        x_prompt = _group(x_prompt, p_prompt[i], lw, conv_w[i], conv_b[i][None, :], filt, filt_bias[i], tailw)
        x_sample = _group(x_sample, p_sample[i], lw, conv_w[i], conv_b[i][None, :], filt, filt_bias[i], tailw)
    return (x_prompt, x_sample)
```

```python
import functools
import math

import jax
import jax.numpy as jnp
from jax import lax
from jax.experimental import pallas as pl
from jax.experimental.pallas import tpu as pltpu

F32 = jnp.float32
BF16 = jnp.bfloat16
HIGHEST = lax.Precision.HIGHEST

LANES = 128
SUBLANES = 8
VMEM_LIMIT = 56 * 1024 * 1024

HY_WIDTH = 512
HY_STREAMS = 3
HY3 = HY_STREAMS * HY_WIDTH
FILT_BANDS = 16
FILT_HIDDEN = 64
DECAY_TARGET = 1e-2
FAST_DECAY_PCT = 0.3
SLOW_DECAY_PCT = 1.5
N_HEADS = 8
QK_NOPE = 64
QK_ROPE = 32
QK_DIM = QK_NOPE + QK_ROPE
V_DIM = 64
Q_LORA = 256
KV_LORA = 128
ROPE_THETA = 10000.0
EPS = 1e-6
HEAD_PAD = LANES
HALF_ROPE = QK_ROPE // 2
INPROJ_SUBTILES = 4


def _rms(x, w):
    ms = jnp.mean(x * x, axis=-1, keepdims=True)
    return x * lax.rsqrt(ms + EPS) * w


def _params(*sem):
    return pltpu.CompilerParams(dimension_semantics=sem, vmem_limit_bytes=VMEM_LIMIT)


def _const_spec(shape):
    nd = len(shape)
    return pl.BlockSpec(shape, lambda *_: (0,) * nd)


def _inproj_body(xp_ref, x_ref, xn_ref, anw_ref, win_ref, cw_ref, cb_ref, qaw_ref, wqb_ref, kvaw_ref, wkvb_ref,
                 qnw_ref, knw_ref, cos_ref, sin_ref, x0_ref, z_ref, q_ref, k_ref, v_ref):
    t = pl.program_id(1)
    tm = x_ref.shape[0]
    has_prev = (t > 0).astype(F32)
    has_next = (t < pl.num_programs(1) - 1).astype(F32)
    xa = jnp.concatenate([xp_ref[...] * has_prev, x_ref[...], xn_ref[...] * has_next], axis=0)
    kw = N_HEADS * HEAD_PAD
    lane = lax.broadcasted_iota(jnp.int32, (1, kw), 1) % (2 * HEAD_PAD)
    ones_col = (lane == V_DIM) | (lane == HEAD_PAD)
    qscale = QK_DIM ** -0.5 * math.log2(math.e)
    o2 = HY3 + Q_LORA
    o3 = o2 + KV_LORA
    o4 = o3 + HEAD_PAD

    def rstd(xh):
        return lax.rsqrt(jnp.sum(xh * xh, axis=-1, keepdims=True) / QK_DIM + EPS)

    def sub_tile(out, sub):
        rows = sub + 2 * SUBLANES
        proj = jnp.dot(_rms(xa[out:out + rows], anw_ref[...]).astype(BF16), win_ref[...],
                       preferred_element_type=F32)
        body = slice(SUBLANES, SUBLANES + sub)
        dst = slice(out, out + sub)

        def stream(s):
            sl = slice(s * HY_WIDTH, (s + 1) * HY_WIDTH)
            h = proj[:, sl]
            dn, up = pltpu.roll(h, 1, 0)[body], pltpu.roll(h, rows - 1, 0)[body]
            return cb_ref[:, sl] + dn * cw_ref[0:1, sl] + h[body] * cw_ref[1:2, sl] + up * cw_ref[2:3, sl]

        x0_ref[dst] = stream(0)
        z_ref[dst] = stream(2) * stream(1)

        q_lat, kv_lat, krope, krope_pi = proj[body, HY3:o2], proj[body, o2:o3], proj[body, o3:o4], proj[body, o4:]
        qf = jnp.dot(_rms(q_lat, qaw_ref[...]).astype(BF16), wqb_ref[...], preferred_element_type=F32)
        kvf = jnp.dot(_rms(kv_lat, kvaw_ref[...]).astype(BF16), wkvb_ref[...], preferred_element_type=F32)
        v_ref[dst] = jnp.where(ones_col, 1.0, kvf[:, kw:]).astype(BF16)

        cosf, sinf = cos_ref[dst], sin_ref[dst]
        cq, sq = cosf * (qnw_ref[0:1] * qscale), sinf * (qnw_ref[1:2] * qscale)
        ck, sk = cosf * knw_ref[0:1], sinf * knw_ref[1:2]
        k_pi = krope_pi * sk
        for h in range(N_HEADS):
            sl = slice(h * HEAD_PAD, (h + 1) * HEAD_PAD)
            xq = qf[:, sl]
            q_pi = qf[:, kw + h * HEAD_PAD:kw + (h + 1) * HEAD_PAD]
            q_ref[dst, sl] = ((xq * cq + q_pi * sq) * rstd(xq)).astype(BF16)
            xk = kvf[:, sl] + krope
            k_ref[dst, sl] = ((xk * ck + k_pi) * rstd(xk)).astype(BF16)

    sub = tm // INPROJ_SUBTILES
    for j in range(INPROJ_SUBTILES):
        sub_tile(j * sub, sub)


def _inproj(x, anw, win, conv_w, conv_b, qaw, wqb, kvaw, wkvb, qnw, knw, cosf, sinf, tm):
    B, L, D = x.shape
    kw = N_HEADS * HEAD_PAD
    nb = tm // SUBLANES
    last = L // SUBLANES - 1
    tok = lambda w: pl.BlockSpec((None, tm, w), lambda b, t: (b, t, 0))
    pos = pl.BlockSpec((tm, HEAD_PAD), lambda b, t: (t, 0))
    consts = [anw, win, conv_w, conv_b, qaw, wqb, kvaw, wkvb, qnw, knw]
    return pl.pallas_call(
        _inproj_body,
        grid=(B, L // tm),
        in_specs=[pl.BlockSpec((None, SUBLANES, D), lambda b, t: (b, jnp.maximum(t * nb - 1, 0), 0)), tok(D),
                  pl.BlockSpec((None, SUBLANES, D), lambda b, t: (b, jnp.minimum((t + 1) * nb, last), 0))]
                 + [_const_spec(c.shape) for c in consts] + [pos, pos],
        out_specs=[tok(HY_WIDTH), tok(HY_WIDTH), tok(kw), tok(kw), tok(kw)],
        out_shape=[jax.ShapeDtypeStruct((B, L, HY_WIDTH), F32)] * 2 + [jax.ShapeDtypeStruct((B, L, kw), BF16)] * 3,
        compiler_params=_params("parallel", "parallel"),
        name="inproj",
    )(x, x, x, *consts, cosf, sinf)


def _filt_body(z_ref, w1_ref, b1_ref, fr_ref, w2_ref, b2_ref, w3_ref, dl_ref, o_ref):
    hr = z_ref.shape[0] // 2
    top, bot = z_ref[:hr], z_ref[hr:]
    fr = fr_ref[...]
    dot = functools.partial(jnp.dot, precision=HIGHEST, preferred_element_type=F32)
    h = jnp.sin(fr * (dot(jnp.concatenate([top, bot], axis=1), w1_ref[...]) + b1_ref[...]))
    h = jnp.sin(fr * (dot(h, w2_ref[...]) + b2_ref[...]))
    h = dot(h, w3_ref[...])
    dl = dl_ref[...]
    o_ref[:hr] = h[:, :HY_WIDTH] * jnp.exp(-top[:, 0:1] * dl)
    o_ref[hr:] = h[:, HY_WIDTH:] * jnp.exp(-bot[:, 0:1] * dl)


def _filt(feat, w1, b1, fr, w2, b2, w3, deltas, tf):
    n2l = feat.shape[0]
    half = n2l // tf // 2
    return pl.pallas_call(
        _filt_body,
        grid=(n2l // tf,),
        in_specs=[pl.BlockSpec((tf, LANES), lambda t: (t, 0)), _const_spec(w1.shape), _const_spec(b1.shape),
                  _const_spec(fr.shape), _const_spec(w2.shape), _const_spec(b2.shape),
                  pl.BlockSpec((None,) + w3.shape[1:], lambda t: (t // half, 0, 0)),
                  _const_spec(deltas.shape)],
        out_specs=pl.BlockSpec((tf, HY_WIDTH), lambda t: (t, 0)),
        out_shape=jax.ShapeDtypeStruct((n2l, HY_WIDTH), F32),
        compiler_params=_params("parallel"),
        name="filt",
    )(feat, w1, b1, fr, w2, b2, w3, deltas)


def _hi_lo(x):
    hi = x.astype(BF16)
    return hi, (x - hi.astype(F32)).astype(BF16)


def _dot3(m_ref, x):
    xh, xl = _hi_lo(x)
    mh = m_ref[0]
    dot = functools.partial(jnp.dot, preferred_element_type=F32)
    return dot(mh, xh) + dot(mh, xl) + dot(m_ref[1], xh)


def _dft1_body(c_ref, x_ref, o_ref):
    k, rb, cw = x_ref.shape
    m = o_ref.shape[0]
    x2 = x_ref.reshape(k * rb, cw)
    o2 = o_ref.reshape(m * rb, cw)
    for s in range(rb):
        o2[pl.ds(s, m, stride=rb), :] = _dot3(c_ref, x2[pl.ds(s, k, stride=rb), :])


def _dft1(cmat, xv, rb, cw):
    P, K, n2, C = xv.shape
    M = cmat.shape[1]
    return pl.pallas_call(
        _dft1_body,
        grid=(P, n2 // rb, C // cw),
        in_specs=[_const_spec(cmat.shape), pl.BlockSpec((None, K, rb, cw), lambda p, i, c: (p, 0, i, c))],
        out_specs=pl.BlockSpec((None, M, rb, cw), lambda p, i, c: (p, 0, i, c)),
        out_shape=jax.ShapeDtypeStruct((P, M, n2, C), F32),
        compiler_params=_params("parallel", "parallel", "parallel"),
        name="dft1",
    )(cmat, xv)


def _cmul(ar, ai, br, bi):
    return ar * br - ai * bi, ar * bi + ai * br


def _fwd2(f_ref, tw_ref, a_ref, k):
    n2, cw = a_ref.shape[2:]
    reps = cw // LANES
    twr = jnp.concatenate([tw_ref[0, k]] * reps, axis=1)
    twi = jnp.concatenate([tw_ref[1, k]] * reps, axis=1)
    ar, ai = _cmul(a_ref[0, k], a_ref[1, k], twr, twi)
    x = _dot3(f_ref, jnp.concatenate([ar, ai], axis=0))
    return x[:n2], x[n2:], twr, twi


def _mid_kernel_body(f_ref, tw_ref, a_ref, o_ref):
    for k in range(a_ref.shape[1]):
        o_ref[0, k], o_ref[1, k], _, _ = _fwd2(f_ref, tw_ref, a_ref, k)


def _mid_body(f_ref, ft_ref, tw_ref, kf_ref, a_ref, o_ref):
    n2 = a_ref.shape[2]
    for k in range(a_ref.shape[1]):
        xr, xi, twr, twi = _fwd2(f_ref, tw_ref, a_ref, k)
        yr, yi = _cmul(xr, xi, kf_ref[0, k], kf_ref[1, k])
        b = _dot3(ft_ref, jnp.concatenate([yr, yi], axis=0))
        o_ref[0, k], o_ref[1, k] = _cmul(b[:n2], b[n2:], twr, -twi)


def _mid(f2, f2_t, tw, kf, a, kb, cw):
    P, _, n1, n2, C = a.shape
    dat = pl.BlockSpec((None, 2, kb, n2, cw), lambda k, c, p: (p, 0, k, 0, c))
    tws = pl.BlockSpec((2, kb, n2, LANES), lambda k, c, p: (0, k, 0, 0))
    mat = _const_spec(f2.shape)
    if kf is None:
        body, ins, specs = _mid_kernel_body, (f2, tw, a), [mat, tws, dat]
    else:
        kspec = pl.BlockSpec((None, 2, kb, n2, cw), lambda k, c, p: (0, 0, k, 0, c))
        body, ins, specs = _mid_body, (f2, f2_t, tw, kf, a), [mat, mat, tws, kspec, dat]
    return pl.pallas_call(
        body,
        grid=(n1 // kb, C // cw, P),
        in_specs=specs,
        out_specs=dat,
        out_shape=jax.ShapeDtypeStruct(a.shape, F32),
        compiler_params=_params("parallel", "parallel", "parallel"),
        name="mid",
    )(*ins)


def _dft_tables(n1, n2):
    n = n1 * n2
    h = n1 // 2
    i1 = jnp.arange(n1, dtype=jnp.int32)
    th = (2.0 * math.pi / n1) * ((i1[:, None] * i1[None, :]) % n1).astype(F32)
    c1, s1 = jnp.cos(th), jnp.sin(th)
    cz = jnp.block([[c1[:, :h], s1[:, :h]], [-s1[:, :h], c1[:, :h]]])
    ck = jnp.concatenate([c1, -s1], axis=0)
    c4 = jnp.block([[c1[:h], -s1[:h]], [s1[:h], c1[:h]]]) * (1.0 / n)
    u = jnp.arange(n2, dtype=jnp.int32)
    ph2 = (2.0 * math.pi / n2) * ((u[:, None] * u[None, :]) % n2).astype(F32)
    c2, s2 = jnp.cos(ph2), jnp.sin(ph2)
    f2 = jnp.block([[c2, s2], [-s2, c2]])
    pht = (2.0 * math.pi / n) * (i1[:, None] * u[None, :]).astype(F32)
    tw = jnp.stack([jnp.cos(pht), -jnp.sin(pht)])
    tw = jnp.broadcast_to(tw[..., None], tw.shape + (LANES,))
    parts = lambda m: jnp.stack(_hi_lo(m))
    return parts(cz), parts(ck), parts(c4), parts(f2), parts(f2.T), tw


def _split(L):
    n = 2 * L
    n1 = 1 << ((n.bit_length() - 1) // 2)
    return n1, n // n1


def _hyena_conv(z, kern, kb, rb, cw):
    B, L, C = z.shape
    n1, n2 = _split(L)
    P = B // 2
    cz, ck, c4, f2, f2_t, tw = _dft_tables(n1, n2)
    ka = _dft1(ck, kern.reshape(1, n1, n2, C), rb, LANES)
    kf = _mid(f2, None, tw, None, ka.reshape(1, 2, n1, n2, C), kb, cw)
    za = _dft1(cz, z.reshape(P, n1, n2, C), rb, LANES)
    zb = _mid(f2, f2_t, tw, kf, za.reshape(P, 2, n1, n2, C), kb, cw)
    return _dft1(c4, zb.reshape(P, 2 * n1, n2, C), rb, LANES).reshape(B, L, C)


def _attn_body(q_ref, k_ref, v_ref, o_ref, vt_scr, s_scr, p_scr, m_scr, acc_scr, *, tk, cb):
    tq = q_ref.shape[0]
    nk = k_ref.shape[0] // tk
    m_scr[...] = jnp.full(m_scr.shape, -jnp.inf, F32)
    acc_scr[...] = jnp.zeros(acc_scr.shape, F32)

    @pl.when(pl.program_id(2) == 0)
    def _():
        for c in range(nk):
            rows = slice(c * tk, (c + 1) * tk)
            vt_scr[:, rows] = v_ref[rows, :].astype(F32).T.astype(BF16)

    qt = q_ref[...].astype(F32).T.astype(BF16)

    def step(j, carry):
        off = pl.multiple_of(j * tk, tk)
        for h in range(2):
            hs = slice(h * HEAD_PAD, (h + 1) * HEAD_PAD)
            s_scr[h] = jnp.dot(k_ref[pl.ds(off, tk), hs], qt[hs], preferred_element_type=F32)
        for h in range(2):
            hs = slice(h * HEAD_PAD, (h + 1) * HEAD_PAD)
            for c in range(tq // cb):
                cols = slice(c * cb, (c + 1) * cb)
                m_old = m_scr[h, :, cols]
                m_new = jnp.maximum(m_old, jnp.max(s_scr[h, :, cols], axis=0, keepdims=True))
                m_scr[h, :, cols] = m_new
                acc_scr[h, :, cols] = acc_scr[h, :, cols] * jnp.exp2(m_old - m_new)
                p_scr[h, :, cols] = jnp.exp2(s_scr[h, :, cols] - m_new).astype(BF16)
            acc_scr[h] += jnp.dot(vt_scr[hs, pl.ds(off, tk)], p_scr[h], preferred_element_type=F32)
        return carry

    lax.fori_loop(0, nk, step, 0, unroll=2)
    a0, a1 = acc_scr[0], acc_scr[1]
    ot = jnp.concatenate([a0[:V_DIM] / a0[V_DIM:V_DIM + 1], a1[V_DIM:] / a1[0:1]], axis=0)
    o_ref[...] = ot.T


def _attn(q, k, v, tq, tk, cb):
    B, L, _ = q.shape
    pair = 2 * HEAD_PAD
    return pl.pallas_call(
        functools.partial(_attn_body, tk=tk, cb=cb),
        grid=(B, N_HEADS // 2, L // tq),
        in_specs=[pl.BlockSpec((None, tq, pair), lambda b, h, i: (b, i, h)),
                  pl.BlockSpec((None, L, pair), lambda b, h, i: (b, 0, h)),
                  pl.BlockSpec((None, L, pair), lambda b, h, i: (b, 0, h))],
        out_specs=pl.BlockSpec((None, tq, 2 * V_DIM), lambda b, h, i: (b, i, h)),
        out_shape=jax.ShapeDtypeStruct((B, L, N_HEADS * V_DIM), F32),
        scratch_shapes=[pltpu.VMEM((pair, L), BF16), pltpu.VMEM((2, tk, tq), F32), pltpu.VMEM((2, tk, tq), BF16),
                        pltpu.VMEM((2, 1, tq), F32), pltpu.VMEM((2, HEAD_PAD, tq), F32)],
        compiler_params=_params("parallel", "parallel", "arbitrary"),
        name="attn",
    )(q, k, v)


def _tail_body(x_ref, yc_ref, x0_ref, z_ref, om_ref, p_ref, fb_ref, hnw_ref, mnw_ref, wo_ref, mlw_ref, w1_ref,
               w2_ref, wg_ref, wp_ref, pnw_ref, o_ref, *, fc):
    dot = functools.partial(jnp.dot, preferred_element_type=F32)
    yh = x0_ref[...] * (yc_ref[...] + z_ref[...] * fb_ref[...])
    a = _rms(yh, hnw_ref[...]).astype(BF16)
    b = _rms(om_ref[...], mnw_ref[...]).astype(BF16)
    x = x_ref[...] + dot(a, wo_ref[:HY_WIDTH, :]) + dot(b, wo_ref[HY_WIDTH:, :])
    xn = _rms(x, mlw_ref[...]).astype(BF16)
    mlp = jnp.zeros_like(x)
    for c in range(w1_ref.shape[1] // fc):
        h = jnp.maximum(dot(xn, w1_ref[:, c * fc:(c + 1) * fc]), 0.0)
        mlp = mlp + dot((h * h).astype(BF16), w2_ref[c * fc:(c + 1) * fc, :])
    x = x + mlp
    gate = jax.nn.sigmoid(dot(x.astype(BF16), wg_ref[...]))
    pp = _rms(dot(p_ref[...].astype(BF16), wp_ref[...]), pnw_ref[...])
    o_ref[...] = x + gate * pp


def _tail(x, yc, x0, z, om, p, fb, hnw, mnw, wo, mlw, w1, w2, wg, wp, pnw, tm, fc):
    B, L, D = x.shape
    tok = lambda a: pl.BlockSpec((None, tm, a.shape[-1]), lambda b, t: (b, t, 0))
    res = lambda a: pl.BlockSpec(a.shape, lambda b, t: (0,) * a.ndim, pipeline_mode=pl.Buffered(1))
    toks = (x, yc, x0, z, om, p)
    consts = (fb, hnw, mnw, wo, mlw, w1, w2, wg, wp, pnw)
    return pl.pallas_call(
        functools.partial(_tail_body, fc=fc),
        grid=(B, L // tm),
        in_specs=[tok(a) for a in toks] + [res(a) for a in consts],
        out_specs=tok(x),
        out_shape=jax.ShapeDtypeStruct((B, L, D), F32),
        compiler_params=_params("parallel", "parallel"),
        name="tail",
    )(*toks, *consts)


def _rope_tables(L):
    inv = ROPE_THETA ** (-(2.0 * jnp.arange(HALF_ROPE, dtype=F32)) / QK_ROPE)
    ang = jnp.arange(L, dtype=F32)[:, None] * inv[None, :]
    cos, sin = jnp.cos(ang), jnp.sin(ang)
    one = jnp.ones((L, QK_NOPE), F32)
    pad = jnp.zeros((L, HEAD_PAD - QK_DIM), F32)
    cosf = jnp.concatenate([one, cos, cos, pad], axis=1)
    sinf = jnp.concatenate([0.0 * one, -sin, sin, pad], axis=1)
    return cosf, sinf


def _pair_weights(w):
    z = jnp.zeros_like(w)
    return jnp.block([[w, z], [z, w]])


def _filter_features(L):
    pos = jnp.concatenate([jnp.arange(L), jnp.arange(L - 1, -1, -1)]).astype(F32)[:, None]
    t = pos / (L - 1)
    w = (2.0 * math.pi) * pos / L
    f = jnp.linspace(1e-4, FILT_BANDS - 1, FILT_BANDS, dtype=F32)[None, :]
    z = jnp.concatenate([t, jnp.cos(f * w), -jnp.sin(f * w)], axis=-1)
    return jnp.pad(z, ((0, 0), (0, LANES - z.shape[1])))


def _decay_rates():
    min_decay = math.log(DECAY_TARGET) / FAST_DECAY_PCT
    max_decay = math.log(DECAY_TARGET) / SLOW_DECAY_PCT
    return jnp.abs(jnp.linspace(min_decay, max_decay, HY_WIDTH, dtype=F32))[None, :]


def _pad_heads(w, width, lead=0):
    w = w.reshape(w.shape[:-1] + (N_HEADS, width))
    pad = [(0, 0)] * (w.ndim - 1) + [(lead, HEAD_PAD - width - lead)]
    return jnp.pad(w, pad).reshape(w.shape[:-2] + (N_HEADS * HEAD_PAD,))


def _rope_partner(w):
    a, b = w[..., QK_NOPE:QK_NOPE + HALF_ROPE], w[..., QK_NOPE + HALF_ROPE:]
    zeros = lambda n: jnp.zeros(w.shape[:-1] + (n,), w.dtype)
    return jnp.concatenate([zeros(QK_NOPE), b, a, zeros(HEAD_PAD - QK_DIM)], axis=-1)


def _layer_weights(attn_norm_w, w_in, q_a_norm_w, w_q_b, kv_a_norm_w, w_kv_b, q_norm_w, k_norm_w):
    d = w_in.shape[0]
    o3 = HY3 + Q_LORA + KV_LORA
    rope_cols = jnp.pad(w_in[:, o3:], ((0, 0), (QK_NOPE, 0)))
    win = jnp.concatenate([w_in[:, :o3], jnp.pad(rope_cols, ((0, 0), (0, HEAD_PAD - QK_DIM))),
                           _rope_partner(rope_cols)], axis=1).astype(BF16)
    wq = w_q_b.reshape(Q_LORA, N_HEADS, QK_DIM)
    wqb = jnp.concatenate([_pad_heads(w_q_b, QK_DIM), _rope_partner(wq).reshape(Q_LORA, N_HEADS * HEAD_PAD)],
                          axis=1).astype(BF16)
    kv = w_kv_b.reshape(KV_LORA, N_HEADS, QK_NOPE + V_DIM)
    wk = _pad_heads(kv[:, :, :QK_NOPE].reshape(KV_LORA, N_HEADS * QK_NOPE), QK_NOPE)
    v4 = kv[:, :, QK_NOPE:].reshape(KV_LORA, N_HEADS // 2, 2, V_DIM)
    zero = jnp.zeros_like(v4[:, :, 0])
    wv = jnp.stack([v4[:, :, 0], zero, zero, v4[:, :, 1]], axis=2).reshape(KV_LORA, N_HEADS * HEAD_PAD)
    wkvb = jnp.concatenate([wk, wv], axis=1).astype(BF16)
    norm2 = lambda w: jnp.stack([jnp.pad(w, (0, HEAD_PAD - QK_DIM)), _rope_partner(w)])
    return (attn_norm_w[None, :], win, q_a_norm_w[None, :], wqb, kv_a_norm_w[None, :], wkvb,
            norm2(q_norm_w), norm2(k_norm_w))


def _group(x, p, lw, conv_w, conv_b, filt, filt_bias, tailw):
    B, L, D = x.shape
    cosf, sinf = _rope_tables(L)
    x0, z, q, k, v = _inproj(x, lw[0], lw[1], conv_w, conv_b, *lw[2:], cosf, sinf, tm=1024)
    kern = _filt(_filter_features(L), *filt, _decay_rates(), tf=512)
    yc = _hyena_conv(z, kern, kb=8, rb=16, cw=HY_WIDTH)
    om = _attn(q, k, v, tq=min(1024, L), tk=min(1024, L), cb=128)
    return _tail(x, yc, x0, z, om, p, filt_bias[None, :], *tailw, tm=512, fc=1024)


def kernel(x_prompt, x_sample, p_prompt, p_sample, attn_norm_w, w_in, conv_w, conv_b, filt_w1, filt_b1, filt_freq, filt_w2, filt_b2, filt_w3, filt_bias, q_a_norm_w, w_q_b, kv_a_norm_w, w_kv_b, q_norm_w, k_norm_w, hy_out_norm_w, mla_out_norm_w, w_out, mlp_norm_w, w_mlp1, w_mlp2, w_ple_gate, w_ple_proj, ple_norm_w):
    depth = w_in.shape[0]
    for i in range(depth):
        lw = _layer_weights(attn_norm_w[i], w_in[i], q_a_norm_w[i], w_q_b[i], kv_a_norm_w[i], w_kv_b[i],
                            q_norm_w[i], k_norm_w[i])
        w1 = jnp.pad(filt_w1[i], ((0, LANES - filt_w1.shape[1]), (0, 0)))
        twice = lambda v: jnp.concatenate([v, v])[None, :]
        w3 = jnp.stack([_pair_weights(filt_w3[i][:, :HY_WIDTH]), _pair_weights(filt_w3[i][:, HY_WIDTH:])])
        filt = (_pair_weights(w1), twice(filt_b1[i]), twice(filt_freq[i]), _pair_weights(filt_w2[i]),
                twice(filt_b2[i]), w3)
        tailw = (hy_out_norm_w[i][None, :], mla_out_norm_w[i][None, :], w_out[i].astype(BF16),
                 mlp_norm_w[i][None, :], w_mlp1[i].astype(BF16), w_mlp2[i].astype(BF16),
                 w_ple_gate[i].astype(BF16), w_ple_proj[i].astype(BF16), ple_norm_w[i][None, :])
        x_prompt = _group(x_prompt, p_prompt[i], lw, conv_w[i], conv_b[i][None, :], filt, filt_bias[i], tailw)
        x_sample = _group(x_sample, p_sample[i], lw, conv_w[i], conv_b[i][None, :], filt, filt_bias[i], tailw)
    return (x_prompt, x_sample)
```
